```python
import jax, jax.numpy as jnp
from jax import lax
import numpy as np

D_MODEL = 1024
BATCH = 8
SEQ = 2048
DEPTH = 4
DEC_BATCH = 128
DEC_SEQ = 8
PAST_LEN = 2048
PAGE_SIZE = 128

MIX_WIDTH = D_MODEL
ATTN_WIDTH = MIX_WIDTH // 2
POOL_WIDTH = MIX_WIDTH - ATTN_WIDTH
HEAD_DIM = 64
N_HEADS = ATTN_WIDTH // HEAD_DIM
POOL_WINDOWS = (2, 4, 8, 16)
N_POOL_GROUPS = len(POOL_WINDOWS)
POOL_GROUP_WIDTH = POOL_WIDTH // N_POOL_GROUPS
POOL_BUF = max(POOL_WINDOWS) - 1
D_FF = 2816
Q_BLOCK = 128
RMS_EPS = 1e-6
FFN_RES_WEIGHT = 0.5
SB_BIAS_INIT = -7.0

kernel_name = "hybrid_stickbreak_pool_macaron_step"


def rmsnorm(x, g):
    xf = x.astype(jnp.float32)
    y = xf * lax.rsqrt(jnp.mean(xf * xf, axis=-1, keepdims=True) + RMS_EPS)
    return y.astype(x.dtype) * g


def swiglu(x, w_gate, w_up, w_down):
    return (jax.nn.silu(x @ w_gate) * (x @ w_up)) @ w_down


def sb_attend(q, k, v, q_pos, k_pos, bias):
    z = jnp.einsum('bhqd,bhkd->bhqk', q, k).astype(jnp.float32) * (HEAD_DIM ** -0.5)
    z = z + bias.astype(jnp.float32)[None, :, None, None]
    mask = k_pos[None, :] < q_pos[:, None]
    log_beta = jax.nn.log_sigmoid(z)
    log_1mb = jnp.where(mask, jax.nn.log_sigmoid(-z), 0.0)
    log_stay = lax.cumsum(log_1mb, axis=3, reverse=True) - log_1mb
    w = jnp.where(mask, jnp.exp(log_beta + log_stay), 0.0)
    return jnp.einsum('bhqk,bhkd->bhqd', w.astype(v.dtype), v)


def sb_prompt(q, k, v, bias):
    B, H, S, Dh = q.shape
    nb = S // Q_BLOCK
    qb = q.reshape(B, H, nb, Q_BLOCK, Dh).transpose(2, 0, 1, 3, 4)
    pos = jnp.arange(S, dtype=jnp.int32)
    posb = pos.reshape(nb, Q_BLOCK)
    ob = lax.map(lambda a: sb_attend(a[0], k, v, a[1], pos, bias), (qb, posb))
    return ob.transpose(1, 2, 0, 3, 4).reshape(B, H, S, Dh)


def pool_mix(u_ext, pos_ext, w_pool, scale):
    B, L, _ = u_ext.shape
    ug = u_ext.reshape(B, L, N_POOL_GROUPS, POOL_GROUP_WIDTH).astype(jnp.float32)
    csum = jnp.concatenate([jnp.zeros((B, 1, N_POOL_GROUPS, POOL_GROUP_WIDTH), jnp.float32),
                            jnp.cumsum(ug, axis=1)], axis=1)
    hi = jnp.arange(1, L + 1, dtype=jnp.int32)
    means = []
    for g, win in enumerate(POOL_WINDOWS):
        lo = jnp.maximum(hi - win, 0)
        cnt = jnp.minimum(pos_ext + 1, win).astype(jnp.float32)
        s = csum[:, hi, g] - csum[:, lo, g]
        means.append(s / cnt[None, :, None])
    pooled = jnp.stack(means, axis=2)
    d = (pooled - ug).astype(u_ext.dtype)
    o = jnp.einsum('blgc,gcd->blgd', d, w_pool).reshape(B, L, POOL_WIDTH)
    return o * scale


def split_proj(h, w_in):
    B, T, _ = h.shape
    proj = h @ w_in
    q, k, v, u = jnp.split(proj, [ATTN_WIDTH, 2 * ATTN_WIDTH, 3 * ATTN_WIDTH], axis=-1)
    shp = (B, T, N_HEADS, HEAD_DIM)
    return q.reshape(shp), k.reshape(shp), v.reshape(shp), u


def to_bhtd(t):
    return t.transpose(0, 2, 1, 3)


def setup_inputs(seed: int = 0) -> dict:
    key = jax.random.key(seed)
    ks = jax.random.split(key, 24)
    n_pages = PAST_LEN // PAGE_SIZE
    n_used = DEC_BATCH * n_pages
    n_pool_pages = n_used + max(n_used // 4, 1)
    f32 = jnp.float32

    def nrm(k, shape, s=1.0):
        return jax.random.normal(k, shape, f32) * s

    def gain(k):
        return 1.0 + 0.05 * nrm(k, (DEPTH, D_MODEL))

    page_table = jax.random.permutation(ks[5], n_pool_pages)[:n_used].astype(jnp.int32).reshape(DEC_BATCH, n_pages)
    return {
        "x_prompt": nrm(ks[0], (BATCH, SEQ, D_MODEL)),
        "x_sample": nrm(ks[1], (DEC_BATCH, DEC_SEQ, D_MODEL)),
        "cache_k": nrm(ks[2], (DEPTH, n_pool_pages, PAGE_SIZE, N_HEADS, HEAD_DIM)),
        "cache_v": nrm(ks[3], (DEPTH, n_pool_pages, PAGE_SIZE, N_HEADS, HEAD_DIM)),
        "state_pool": nrm(ks[4], (DEPTH, DEC_BATCH, POOL_BUF, POOL_WIDTH)),
        "page_table": page_table,
        "w_in": nrm(ks[6], (DEPTH, D_MODEL, 3 * ATTN_WIDTH + POOL_WIDTH), D_MODEL ** -0.5),
        "w_out": nrm(ks[7], (DEPTH, MIX_WIDTH, D_MODEL), MIX_WIDTH ** -0.5),
        "sb_bias": SB_BIAS_INIT + 0.3 * nrm(ks[22], (DEPTH, N_HEADS)),
        "pool_w": nrm(ks[8], (DEPTH, N_POOL_GROUPS, POOL_GROUP_WIDTH, POOL_GROUP_WIDTH), POOL_GROUP_WIDTH ** -0.5),
        "pool_scale": 1.0 + 0.1 * nrm(ks[9], (DEPTH, POOL_WIDTH)),
        "ffn1_pre_g": gain(ks[10]),
        "ffn1_post_g": gain(ks[11]),
        "ffn1_w_gate": nrm(ks[12], (DEPTH, D_MODEL, D_FF), D_MODEL ** -0.5),
        "ffn1_w_up": nrm(ks[13], (DEPTH, D_MODEL, D_FF), D_MODEL ** -0.5),
        "ffn1_w_down": nrm(ks[14], (DEPTH, D_FF, D_MODEL), D_FF ** -0.5),
        "mix_pre_g": gain(ks[15]),
        "mix_post_g": gain(ks[16]),
        "ffn2_pre_g": gain(ks[17]),
        "ffn2_post_g": gain(ks[18]),
        "ffn2_w_gate": nrm(ks[19], (DEPTH, D_MODEL, D_FF), D_MODEL ** -0.5),
        "ffn2_w_up": nrm(ks[20], (DEPTH, D_MODEL, D_FF), D_MODEL ** -0.5),
        "ffn2_w_down": nrm(ks[21], (DEPTH, D_FF, D_MODEL), D_FF ** -0.5),
    }


def reference(x_prompt, x_sample, cache_k, cache_v, state_pool, page_table,
              w_in, w_out, sb_bias, pool_w, pool_scale,
              ffn1_pre_g, ffn1_post_g, ffn1_w_gate, ffn1_w_up, ffn1_w_down,
              mix_pre_g, mix_post_g,
              ffn2_pre_g, ffn2_post_g, ffn2_w_gate, ffn2_w_up, ffn2_w_down):
    B, S, _ = x_prompt.shape
    DB, T, _ = x_sample.shape
    past_len = page_table.shape[1] * cache_k.shape[2]

    q_pos_s = past_len + jnp.arange(T, dtype=jnp.int32)
    k_pos_s = jnp.arange(past_len + T, dtype=jnp.int32)
    pos_p = jnp.arange(S, dtype=jnp.int32)
    pos_ext_s = past_len - POOL_BUF + jnp.arange(POOL_BUF + T, dtype=jnp.int32)

    xp, xs = x_prompt, x_sample
    k_p_rows, v_p_rows, pool_p_rows = [], [], []
    k_s_rows, v_s_rows, pool_s_rows = [], [], []
    for l in range(DEPTH):
        xp = xp + FFN_RES_WEIGHT * rmsnorm(swiglu(rmsnorm(xp, ffn1_pre_g[l]), ffn1_w_gate[l], ffn1_w_up[l], ffn1_w_down[l]), ffn1_post_g[l])
        xs = xs + FFN_RES_WEIGHT * rmsnorm(swiglu(rmsnorm(xs, ffn1_pre_g[l]), ffn1_w_gate[l], ffn1_w_up[l], ffn1_w_down[l]), ffn1_post_g[l])

        q, k, v, u = split_proj(rmsnorm(xp, mix_pre_g[l]), w_in[l])
        o_a = to_bhtd(sb_prompt(to_bhtd(q), to_bhtd(k), to_bhtd(v), sb_bias[l])).reshape(B, S, ATTN_WIDTH)
        o_b = pool_mix(u, pos_p, pool_w[l], pool_scale[l])
        xp = xp + rmsnorm(jnp.concatenate([o_a, o_b], axis=-1) @ w_out[l], mix_post_g[l])
        k_p_rows.append(k)
        v_p_rows.append(v)
        pool_p_rows.append(u[:, S - POOL_BUF:])

        q, k, v, u = split_proj(rmsnorm(xs, mix_pre_g[l]), w_in[l])
        past_k = cache_k[l][page_table].reshape(DB, past_len, N_HEADS, HEAD_DIM).astype(k.dtype)
        past_v = cache_v[l][page_table].reshape(DB, past_len, N_HEADS, HEAD_DIM).astype(v.dtype)
        k_all = jnp.concatenate([past_k, k], axis=1)
        v_all = jnp.concatenate([past_v, v], axis=1)
        o_a = to_bhtd(sb_attend(to_bhtd(q), to_bhtd(k_all), to_bhtd(v_all), q_pos_s, k_pos_s, sb_bias[l])).reshape(DB, T, ATTN_WIDTH)
        u_ext = jnp.concatenate([state_pool[l].astype(u.dtype), u], axis=1)
        o_b = pool_mix(u_ext, pos_ext_s, pool_w[l], pool_scale[l])[:, POOL_BUF:]
        xs = xs + rmsnorm(jnp.concatenate([o_a, o_b], axis=-1) @ w_out[l], mix_post_g[l])
        k_s_rows.append(k)
        v_s_rows.append(v)
        pool_s_rows.append(u_ext[:, T:])

        xp = xp + FFN_RES_WEIGHT * rmsnorm(swiglu(rmsnorm(xp, ffn2_pre_g[l]), ffn2_w_gate[l], ffn2_w_up[l], ffn2_w_down[l]), ffn2_post_g[l])
        xs = xs + FFN_RES_WEIGHT * rmsnorm(swiglu(rmsnorm(xs, ffn2_pre_g[l]), ffn2_w_gate[l], ffn2_w_up[l], ffn2_w_down[l]), ffn2_post_g[l])

    k_prompt = jnp.stack(k_p_rows)
    v_prompt = jnp.stack(v_p_rows)
    pool_prompt = jnp.stack(pool_p_rows)
    k_sample = jnp.stack(k_s_rows)
    v_sample = jnp.stack(v_s_rows)
    pool_sample = jnp.stack(pool_s_rows)
    return (xp, xs, k_prompt, v_prompt, pool_prompt, k_sample, v_sample, pool_sample)
```

```python
import functools

import jax
import jax.numpy as jnp
from jax import lax
from jax.experimental import pallas as pl
from jax.experimental.pallas import tpu as pltpu

D_MODEL = 1024
D_FF = 2816
ATTN_WIDTH = 512
POOL_WIDTH = 512
HEAD_DIM = 64
N_HEADS = 8
POOL_WINDOWS = (2, 4, 8, 16)
POOL_GROUP_WIDTH = 128
POOL_BUF = 15
RMS_EPS = 1e-6
FFN_RES_WEIGHT = 0.5

LANES = 128
HALO = 16
TM = 512
FF_CHUNK = 256
TQ = 256
TK = 256
KB = 256

F32 = jnp.float32
BF16 = jnp.bfloat16


def _rms(x, g):
    ms = jnp.mean(x * x, axis=-1, keepdims=True)
    return (x * lax.rsqrt(ms + RMS_EPS)) * g


def _log_sigmoids(z):
    sp = jnp.log1p(jnp.exp(-jnp.abs(z)))
    mn = jnp.minimum(z, 0.0)
    return mn - sp, (mn - z) - sp


def _split_bf16(x):
    hi = x.astype(BF16)
    lo = (x - hi.astype(F32)).astype(BF16)
    return hi, lo


def _later_keys(n):
    return (lax.broadcasted_iota(jnp.int32, (n, n), 0)
            > lax.broadcasted_iota(jnp.int32, (n, n), 1)).astype(BF16)


def _sum_over_later_keys(log_1mb, later):
    hi, lo = _split_bf16(log_1mb)
    return (jnp.dot(hi, later, preferred_element_type=F32)
            + jnp.dot(lo, later, preferred_element_type=F32))


def _ffn_body(x_ref, pre_ref, wg_ref, wu_ref, wd_ref, post_ref, o_ref, a_ref):
    x = x_ref[...]
    h = _rms(x, pre_ref[...]).astype(BF16)
    for c in range(D_FF // FF_CHUNK):
        cols = pl.ds(c * FF_CHUNK, FF_CHUNK)
        g = jnp.dot(h, wg_ref[:, cols], preferred_element_type=F32)
        u = jnp.dot(h, wu_ref[:, cols], preferred_element_type=F32)
        a_ref[:, cols] = (g / (1.0 + jnp.exp(-g)) * u).astype(BF16)
    y = jnp.dot(a_ref[...], wd_ref[...], preferred_element_type=F32)
    o_ref[...] = x + FFN_RES_WEIGHT * _rms(y, post_ref[...])


def _resident(shape):
    return pl.BlockSpec(shape, lambda i: (0,) * len(shape), pipeline_mode=pl.Buffered(1))


def _ffn(x, pre_g, wg, wu, wd, post_g):
    t = x.shape[0]
    return pl.pallas_call(
        _ffn_body,
        grid=(t // TM,),
        in_specs=[
            pl.BlockSpec((TM, D_MODEL), lambda i: (i, 0)),
            _resident((1, D_MODEL)),
            _resident((D_MODEL, D_FF)),
            _resident((D_MODEL, D_FF)),
            _resident((D_FF, D_MODEL)),
            _resident((1, D_MODEL)),
        ],
        out_specs=pl.BlockSpec((TM, D_MODEL), lambda i: (i, 0)),
        out_shape=jax.ShapeDtypeStruct((t, D_MODEL), F32),
        scratch_shapes=[pltpu.VMEM((TM, D_FF), BF16)],
        compiler_params=pltpu.CompilerParams(
            dimension_semantics=("arbitrary",), vmem_limit_bytes=48 * 1024 * 1024),
        name="ffn",
    )(x, pre_g, wg, wu, wd, post_g)


def _inproj_body(x_ref, g_ref, w_ref, pw_ref, ps_ref,
                 q_ref, k_ref, v_ref, kb_ref, vb_ref, u_ref, ob_ref, halo_ref, *, tiles_per_seq):
    i = pl.program_id(0)
    h = _rms(x_ref[...], g_ref[...]).astype(BF16)
    proj = jnp.dot(h, w_ref[...], preferred_element_type=F32)
    q = proj[:, 0:ATTN_WIDTH]
    k = proj[:, ATTN_WIDTH:2 * ATTN_WIDTH]
    v = proj[:, 2 * ATTN_WIDTH:3 * ATTN_WIDTH]
    u = proj[:, 3 * ATTN_WIDTH:]
    q_ref[...] = (q * (HEAD_DIM ** -0.5)).astype(BF16)
    k_ref[...] = k
    v_ref[...] = v
    kb_ref[...] = k.astype(BF16)
    vb_ref[...] = v.astype(BF16)
    u_ref[...] = u

    tile_in_seq = lax.rem(i, tiles_per_seq)

    @pl.when(tile_in_seq == 0)
    def _():
        halo_ref[...] = jnp.zeros_like(halo_ref)

    pos = tile_in_seq * TM + lax.broadcasted_iota(jnp.int32, (TM, LANES), 0)
    for g, win in enumerate(POOL_WINDOWS):
        cols = slice(g * POOL_GROUP_WIDTH, (g + 1) * POOL_GROUP_WIDTH)
        ug = u[:, cols]
        s = jnp.concatenate([halo_ref[:, cols], ug], axis=0)
        step = 1
        while step < win:
            s = s + pltpu.roll(s, step, axis=0)
            step *= 2
        cnt = jnp.minimum(pos + 1, win).astype(F32)
        d = (s[HALO:] / cnt - ug).astype(BF16)
        o = jnp.dot(d, pw_ref[g], preferred_element_type=F32)
        ob_ref[:, cols] = (o * ps_ref[:, cols]).astype(BF16)
    halo_ref[...] = u[TM - HALO:, :]


def _inproj(x, g, w_in, pool_w, pool_scale, seq_len):
    t = x.shape[0]
    wide = 3 * ATTN_WIDTH + POOL_WIDTH
    row = lambda i: (i, 0)
    spec512 = pl.BlockSpec((TM, ATTN_WIDTH), row)
    return pl.pallas_call(
        functools.partial(_inproj_body, tiles_per_seq=seq_len // TM),
        grid=(t // TM,),
        in_specs=[
            pl.BlockSpec((TM, D_MODEL), row),
            _resident((1, D_MODEL)),
            _resident((D_MODEL, wide)),
            _resident((len(POOL_WINDOWS), POOL_GROUP_WIDTH, POOL_GROUP_WIDTH)),
            _resident((1, POOL_WIDTH)),
        ],
        out_specs=[spec512] * 7,
        out_shape=[
            jax.ShapeDtypeStruct((t, ATTN_WIDTH), BF16),
            jax.ShapeDtypeStruct((t, ATTN_WIDTH), F32),
            jax.ShapeDtypeStruct((t, ATTN_WIDTH), F32),
            jax.ShapeDtypeStruct((t, ATTN_WIDTH), BF16),
            jax.ShapeDtypeStruct((t, ATTN_WIDTH), BF16),
            jax.ShapeDtypeStruct((t, POOL_WIDTH), F32),
            jax.ShapeDtypeStruct((t, POOL_WIDTH), BF16),
        ],
        scratch_shapes=[pltpu.VMEM((HALO, POOL_WIDTH), F32)],
        compiler_params=pltpu.CompilerParams(
            dimension_semantics=("arbitrary",), vmem_limit_bytes=48 * 1024 * 1024),
        name="inproj",
    )(x, g, w_in, pool_w, pool_scale)


def _attn_body(bias_ref, q_ref, k_ref, v_ref, o_ref):
    hp = pl.program_id(1)
    i = pl.program_id(2)
    q = q_ref[...]
    lane = lax.broadcasted_iota(jnp.int32, (TQ, LANES), 1)
    first = lane < HEAD_DIM
    zeros_q = jnp.zeros_like(q)
    qs = (jnp.where(first, q, zeros_q), jnp.where(first, zeros_q, q))
    biases = (bias_ref[2 * hp], bias_ref[2 * hp + 1])
    causal = (lax.broadcasted_iota(jnp.int32, (TQ, TK), 1)
              < lax.broadcasted_iota(jnp.int32, (TQ, TK), 0))
    later = _later_keys(TK)

    def tile(j, masked, state):
        rows = pl.ds(pl.multiple_of(j * TK, TK), TK)
        k = k_ref[rows, :]
        v = v_ref[rows, :]
        new_state = []
        for hh in range(2):
            acc, carry = state[hh]
            z = lax.dot_general(qs[hh], k, (((1,), (1,)), ((), ())),
                                preferred_element_type=F32) + biases[hh]
            log_beta, log_1mb = _log_sigmoids(z)
            if masked:
                log_1mb = jnp.where(causal, log_1mb, 0.0)
            stay = _sum_over_later_keys(log_1mb, later) + carry
            w = jnp.exp(log_beta + stay)
            if masked:
                w = jnp.where(causal, w, 0.0)
            acc = acc + jnp.dot(w.astype(BF16), v, preferred_element_type=F32)
            carry = carry + jnp.sum(log_1mb, axis=1, keepdims=True)
            new_state.append((acc, carry))
        return tuple(new_state)

    init = tuple((jnp.zeros((TQ, LANES), F32), jnp.zeros((TQ, 1), F32)) for _ in range(2))
    state = tile(i, True, init)
    state = lax.fori_loop(0, i, lambda jj, st: tile(i - 1 - jj, False, st), state)
    o_ref[...] = jnp.where(first, state[0][0], state[1][0]).astype(BF16)


def _prompt_attn(q, kb, vb, bias, batch, seq_len):
    nq = seq_len // TQ
    return pl.pallas_call(
        _attn_body,
        grid=(batch, ATTN_WIDTH // LANES, nq),
        in_specs=[
            pl.BlockSpec(memory_space=pltpu.SMEM),
            pl.BlockSpec((TQ, LANES), lambda b, hp, i: (b * nq + i, hp)),
            pl.BlockSpec((seq_len, LANES), lambda b, hp, i: (b, hp)),
            pl.BlockSpec((seq_len, LANES), lambda b, hp, i: (b, hp)),
        ],
        out_specs=pl.BlockSpec((TQ, LANES), lambda b, hp, i: (b * nq + i, hp)),
        out_shape=jax.ShapeDtypeStruct((batch * seq_len, ATTN_WIDTH), BF16),
        compiler_params=pltpu.CompilerParams(
            dimension_semantics=("arbitrary", "arbitrary", "arbitrary"),
            vmem_limit_bytes=48 * 1024 * 1024),
        name="prompt_attn",
    )(bias, q, kb, vb)


def _decode_body(pt_ref, q_ref, kn_ref, vn_ref, un_ref, st_ref, bias_ref, pw_ref, ps_ref,
                 ck_hbm, cv_hbm, o_ref, kbuf, vbuf, sem, *, layer, n_pages, page_size):
    b = pl.program_id(0)
    nb = pl.num_programs(0)
    slot = lax.rem(b, 2)
    dec_seq = q_ref.shape[0]
    n_rows = N_HEADS * dec_seq
    pages_per_block = KB // page_size

    def page_copies(seq, sl):
        cps = []
        for p in range(n_pages):
            page = pt_ref[seq, p]
            cps.append(pltpu.make_async_copy(ck_hbm.at[layer, page], kbuf.at[sl, p], sem.at[0, sl]))
            cps.append(pltpu.make_async_copy(cv_hbm.at[layer, page], vbuf.at[sl, p], sem.at[1, sl]))
        return cps

    @pl.when(b == 0)
    def _():
        for cp in page_copies(0, 0):
            cp.start()

    @pl.when(b + 1 < nb)
    def _():
        for cp in page_copies(b + 1, 1 - slot):
            cp.start()

    q = q_ref[...].astype(F32)
    sel = (lax.broadcasted_iota(jnp.int32, (n_rows, dec_seq), 0) % dec_seq
           == lax.broadcasted_iota(jnp.int32, (n_rows, dec_seq), 1)).astype(F32)
    q_rows = jnp.dot(sel, q, preferred_element_type=F32)
    row_head = lax.broadcasted_iota(jnp.int32, (n_rows, ATTN_WIDTH), 0) // dec_seq
    col_head = lax.broadcasted_iota(jnp.int32, (n_rows, ATTN_WIDTH), 1) // HEAD_DIM
    qbd = jnp.where(row_head == col_head, q_rows, 0.0).astype(BF16)
    bias = bias_ref[...]

    pad = jnp.zeros((LANES - dec_seq, ATTN_WIDTH), F32)
    kn = jnp.concatenate([kn_ref[...], pad], axis=0).astype(BF16)
    vn = jnp.concatenate([vn_ref[...], pad], axis=0).astype(BF16)
    zn = lax.dot_general(qbd, kn, (((1,), (1,)), ((), ())),
                         preferred_element_type=F32) + bias[:, :LANES]
    mask_n = (lax.broadcasted_iota(jnp.int32, (n_rows, LANES), 1)
              < lax.broadcasted_iota(jnp.int32, (n_rows, LANES), 0) % dec_seq)
    log_beta_n, log_1mb_n = _log_sigmoids(zn)
    log_1mb_n = jnp.where(mask_n, log_1mb_n, 0.0)
    stay_n = _sum_over_later_keys(log_1mb_n, _later_keys(LANES))
    w_n = jnp.where(mask_n, jnp.exp(log_beta_n + stay_n), 0.0).astype(BF16)
    o_new = jnp.dot(w_n, vn, preferred_element_type=F32)
    carry0 = jnp.sum(log_1mb_n, axis=1, keepdims=True)

    for cp in page_copies(b, slot):
        cp.wait()

    later = _later_keys(KB)
    w_pad = jnp.zeros((LANES - n_rows, KB), BF16)

    def block(jj, state):
        acc_t, carry = state
        first_page = n_pages - (jj + 1) * pages_per_block
        kblk = jnp.concatenate([kbuf[slot, first_page + p] for p in range(pages_per_block)],
                               axis=1).astype(BF16)
        vblk = jnp.concatenate([vbuf[slot, first_page + p] for p in range(pages_per_block)],
                               axis=1).astype(BF16)
        z = jnp.dot(qbd, kblk, preferred_element_type=F32) + bias
        log_beta, log_1mb = _log_sigmoids(z)
        stay = _sum_over_later_keys(log_1mb, later) + carry
        w = jnp.concatenate([jnp.exp(log_beta + stay).astype(BF16), w_pad], axis=0)
        acc_t = acc_t + lax.dot_general(vblk, w, (((1,), (1,)), ((), ())),
                                        preferred_element_type=F32)
        carry = carry + jnp.sum(log_1mb, axis=1, keepdims=True)
        return acc_t, carry

    acc_t, _ = lax.fori_loop(0, n_pages // pages_per_block, block,
                             (jnp.zeros((ATTN_WIDTH, LANES), F32), carry0))
    acc = acc_t.T[:n_rows] + o_new

    lane = lax.broadcasted_iota(jnp.int32, (dec_seq, LANES), 1)
    for cb in range(ATTN_WIDTH // LANES):
        cols = slice(cb * LANES, (cb + 1) * LANES)
        h0 = 2 * cb
        a0 = acc[h0 * dec_seq:(h0 + 1) * dec_seq, cols]
        a1 = acc[(h0 + 1) * dec_seq:(h0 + 2) * dec_seq, cols]
        o_ref[:, cols] = jnp.where(lane < HEAD_DIM, a0, a1)

    un = un_ref[...]
    ext = jnp.concatenate([st_ref[...], un], axis=0)
    for g, win in enumerate(POOL_WINDOWS):
        cols = slice(g * POOL_GROUP_WIDTH, (g + 1) * POOL_GROUP_WIDTH)
        s = ext[:, cols]
        step = 1
        while step < win:
            s = s + pltpu.roll(s, step, axis=0)
            step *= 2
        d = (s[HALO:] / float(win) - un[:, cols]).astype(BF16).astype(F32)
        o = jnp.dot(d, pw_ref[g], preferred_element_type=F32)
        o_ref[:, ATTN_WIDTH + g * POOL_GROUP_WIDTH:ATTN_WIDTH + (g + 1) * POOL_GROUP_WIDTH] = (
            o * ps_ref[:, cols])


def _decode(page_table, q, kn, vn, un, st, bias_rows, pool_w, pool_scale, cache_kt, cache_vt, layer):
    nb, dec_seq, _ = q.shape
    n_pages = page_table.shape[1]
    page_size = cache_kt.shape[-1]
    n_rows = N_HEADS * dec_seq
    assert KB % page_size == 0 and n_pages % (KB // page_size) == 0 and page_size == LANES
    seq = lambda b, pt: (b, 0, 0)
    const2 = lambda b, pt: (0, 0)
    const3 = lambda b, pt: (0, 0, 0)
    grid_spec = pltpu.PrefetchScalarGridSpec(
        num_scalar_prefetch=1,
        grid=(nb,),
        in_specs=[
            pl.BlockSpec((None, dec_seq, ATTN_WIDTH), seq),
            pl.BlockSpec((None, dec_seq, ATTN_WIDTH), seq),
            pl.BlockSpec((None, dec_seq, ATTN_WIDTH), seq),
            pl.BlockSpec((None, dec_seq, POOL_WIDTH), seq),
            pl.BlockSpec((None, HALO, POOL_WIDTH), seq),
            pl.BlockSpec((n_rows, KB), const2),
            pl.BlockSpec((len(POOL_WINDOWS), POOL_GROUP_WIDTH, POOL_GROUP_WIDTH), const3),
            pl.BlockSpec((1, POOL_WIDTH), const2),
            pl.BlockSpec(memory_space=pl.ANY),
            pl.BlockSpec(memory_space=pl.ANY),
        ],
        out_specs=pl.BlockSpec((None, dec_seq, ATTN_WIDTH + POOL_WIDTH), seq),
        scratch_shapes=[
            pltpu.VMEM((2, n_pages, ATTN_WIDTH, page_size), F32),
            pltpu.VMEM((2, n_pages, ATTN_WIDTH, page_size), F32),
            pltpu.SemaphoreType.DMA((2, 2)),
        ],
    )
    return pl.pallas_call(
        functools.partial(_decode_body, layer=layer, n_pages=n_pages, page_size=page_size),
        grid_spec=grid_spec,
        out_shape=jax.ShapeDtypeStruct((nb, dec_seq, ATTN_WIDTH + POOL_WIDTH), F32),
        compiler_params=pltpu.CompilerParams(
            dimension_semantics=("arbitrary",), vmem_limit_bytes=48 * 1024 * 1024),
        name="sample_attn",
    )(page_table, q, kn, vn, un, st, bias_rows, pool_w, pool_scale, cache_kt, cache_vt)


def _outproj_body(x_ref, m_ref, w_ref, g_ref, o_ref):
    y = jnp.dot(m_ref[...], w_ref[...], preferred_element_type=F32)
    o_ref[...] = x_ref[...] + _rms(y, g_ref[...])


def _outproj(x, mix, w_out, post_g):
    t = x.shape[0]
    row = lambda i: (i, 0)
    return pl.pallas_call(
        _outproj_body,
        grid=(t // TM,),
        in_specs=[
            pl.BlockSpec((TM, D_MODEL), row),
            pl.BlockSpec((TM, D_MODEL), row),
            _resident((D_MODEL, D_MODEL)),
            _resident((1, D_MODEL)),
        ],
        out_specs=pl.BlockSpec((TM, D_MODEL), row),
        out_shape=jax.ShapeDtypeStruct((t, D_MODEL), F32),
        compiler_params=pltpu.CompilerParams(
            dimension_semantics=("arbitrary",), vmem_limit_bytes=48 * 1024 * 1024),
        name="outproj",
    )(x, mix, w_out, post_g)


def kernel(x_prompt, x_sample, cache_k, cache_v, state_pool, page_table, w_in, w_out, sb_bias, pool_w, pool_scale, ffn1_pre_g, ffn1_post_g, ffn1_w_gate, ffn1_w_up, ffn1_w_down, mix_pre_g, mix_post_g, ffn2_pre_g, ffn2_post_g, ffn2_w_gate, ffn2_w_up, ffn2_w_down):
    batch, seq_len, _ = x_prompt.shape
    dec_batch, dec_seq, _ = x_sample.shape
    depth = w_in.shape[0]
    tp = batch * seq_len
    ts = dec_batch * dec_seq
    assert seq_len % TM == 0 and seq_len % TQ == 0 and (tp + ts) % TM == 0
    assert N_HEADS * dec_seq <= LANES

    n_pool_pages, page_size = cache_k.shape[1], cache_k.shape[2]
    ckt = cache_k.transpose(0, 1, 3, 4, 2).reshape(depth, n_pool_pages, ATTN_WIDTH, page_size)
    cvt = cache_v.transpose(0, 1, 3, 4, 2).reshape(depth, n_pool_pages, ATTN_WIDTH, page_size)
    st_pad = jnp.pad(state_pool, ((0, 0), (0, 0), (HALO - POOL_BUF, 0), (0, 0)))

    x = jnp.concatenate([x_prompt.reshape(tp, D_MODEL), x_sample.reshape(ts, D_MODEL)], axis=0)
    row = lambda a, l: a[l][None, :]
    outs = [[] for _ in range(6)]
    for l in range(depth):
        x = _ffn(x, row(ffn1_pre_g, l), ffn1_w_gate[l].astype(BF16), ffn1_w_up[l].astype(BF16),
                 ffn1_w_down[l].astype(BF16), row(ffn1_post_g, l))

        pw = pool_w[l].astype(BF16)
        ps = row(pool_scale, l)
        q, k, v, kb, vb, u, ob = _inproj(x, row(mix_pre_g, l), w_in[l].astype(BF16), pw, ps, seq_len)
        oa = _prompt_attn(q, kb, vb, sb_bias[l], batch, seq_len)

        bias_rows = jnp.broadcast_to(jnp.repeat(sb_bias[l], dec_seq)[:, None], (N_HEADS * dec_seq, KB))
        per_seq = lambda a: a[tp:].reshape(dec_batch, dec_seq, a.shape[-1])
        dec = _decode(page_table, per_seq(q), per_seq(k), per_seq(v), per_seq(u), st_pad[l],
                      bias_rows, pw.astype(F32), ps, ckt, cvt, l)

        mix = jnp.concatenate(
            [jnp.concatenate([oa, ob[:tp]], axis=1), dec.reshape(ts, D_MODEL).astype(BF16)], axis=0)
        x = _outproj(x, mix, w_out[l].astype(BF16), row(mix_post_g, l))

        x = _ffn(x, row(ffn2_pre_g, l), ffn2_w_gate[l].astype(BF16), ffn2_w_up[l].astype(BF16),
                 ffn2_w_down[l].astype(BF16), row(ffn2_post_g, l))

        outs[0].append(k[:tp].reshape(batch, seq_len, N_HEADS, HEAD_DIM))
        outs[1].append(v[:tp].reshape(batch, seq_len, N_HEADS, HEAD_DIM))
        outs[2].append(u[:tp].reshape(batch, seq_len, POOL_WIDTH)[:, seq_len - POOL_BUF:])
        outs[3].append(k[tp:].reshape(dec_batch, dec_seq, N_HEADS, HEAD_DIM))
        outs[4].append(v[tp:].reshape(dec_batch, dec_seq, N_HEADS, HEAD_DIM))
        u_s = u[tp:].reshape(dec_batch, dec_seq, POOL_WIDTH)
        outs[5].append(jnp.concatenate([state_pool[l], u_s], axis=1)[:, dec_seq:])

    y_prompt = x[:tp].reshape(batch, seq_len, D_MODEL)
    y_sample = x[tp:].reshape(dec_batch, dec_seq, D_MODEL)
    return (y_prompt, y_sample) + tuple(jnp.stack(o) for o in outs)
```

```python
import functools

import jax
import jax.numpy as jnp
from jax import lax
from jax.experimental import pallas as pl
from jax.experimental.pallas import tpu as pltpu

D_MODEL = 1024
D_FF = 2816
ATTN_WIDTH = 512
POOL_WIDTH = 512
HEAD_DIM = 64
N_HEADS = 8
POOL_WINDOWS = (2, 4, 8, 16)
POOL_GROUP_WIDTH = 128
POOL_BUF = 15
RMS_EPS = 1e-6
LOG2E = 1.4426950408889634
FFN_RES_WEIGHT = 0.5

LANES = 128
HALO = 16
TM = 512
FF_CHUNK = 256
TQ = 256
TK = 256
KB = 256
HEAD_PAIRS = 4

F32 = jnp.float32
BF16 = jnp.bfloat16


def _rms(x, g):
    ms = jnp.mean(x * x, axis=-1, keepdims=True)
    return (x * lax.rsqrt(ms + RMS_EPS)) * g


def _log2_sigmoids(z2):
    sign_bit = jnp.uint32(0x80000000)
    neg_abs = lax.bitcast_convert_type(lax.bitcast_convert_type(z2, jnp.uint32) | sign_bit, F32)
    log_beta = jnp.minimum(z2, 0.0) - jnp.log2(1.0 + jnp.exp2(neg_abs))
    return log_beta, log_beta - z2


def _split_bf16(x):
    hi = x.astype(BF16)
    lo = (x - hi.astype(F32)).astype(BF16)
    return hi, lo


def _later_keys(n):
    return (lax.broadcasted_iota(jnp.int32, (n, n), 0)
            > lax.broadcasted_iota(jnp.int32, (n, n), 1)).astype(BF16)


def _sum_over_later_keys(log_1mb, later):
    hi, lo = _split_bf16(log_1mb)
    return (jnp.dot(hi, later, preferred_element_type=F32)
            + jnp.dot(lo, later, preferred_element_type=F32))


def _ffn_body(x_ref, pre_ref, wg_ref, wu_ref, wd_ref, post_ref, o_ref, a_ref):
    x = x_ref[...]
    h = _rms(x, pre_ref[...]).astype(BF16)
    for c in range(D_FF // FF_CHUNK):
        cols = pl.ds(c * FF_CHUNK, FF_CHUNK)
        g = jnp.dot(h, wg_ref[:, cols], preferred_element_type=F32)
        u = jnp.dot(h, wu_ref[:, cols], preferred_element_type=F32)
        a_ref[:, cols] = (g / (1.0 + jnp.exp(-g)) * u).astype(BF16)
    y = jnp.dot(a_ref[...], wd_ref[...], preferred_element_type=F32)
    o_ref[...] = x + FFN_RES_WEIGHT * _rms(y, post_ref[...])


def _resident(shape):
    return pl.BlockSpec(shape, lambda i: (0,) * len(shape), pipeline_mode=pl.Buffered(1))


def _ffn(x, pre_g, wg, wu, wd, post_g):
    t = x.shape[0]
    return pl.pallas_call(
        _ffn_body,
        grid=(t // TM,),
        in_specs=[
            pl.BlockSpec((TM, D_MODEL), lambda i: (i, 0)),
            _resident((1, D_MODEL)),
            _resident((D_MODEL, D_FF)),
            _resident((D_MODEL, D_FF)),
            _resident((D_FF, D_MODEL)),
            _resident((1, D_MODEL)),
        ],
        out_specs=pl.BlockSpec((TM, D_MODEL), lambda i: (i, 0)),
        out_shape=jax.ShapeDtypeStruct((t, D_MODEL), F32),
        scratch_shapes=[pltpu.VMEM((TM, D_FF), BF16)],
        compiler_params=pltpu.CompilerParams(
            dimension_semantics=("arbitrary",), vmem_limit_bytes=48 * 1024 * 1024),
        name="ffn",
    )(x, pre_g, wg, wu, wd, post_g)


def _inproj_body(x_ref, g_ref, w_ref, pw_ref, ps_ref, kt_in_ref, vt_in_ref,
                 q_ref, kb_ref, vb_ref, ob_ref, kt_ref, vt_ref, tail_ref,
                 qs_ref, ks_ref, vs_ref, us_ref, halo_ref, *, tiles_per_seq, prompt_tiles):
    del kt_in_ref, vt_in_ref
    i = pl.program_id(0)
    h = _rms(x_ref[...], g_ref[...]).astype(BF16)
    proj = jnp.dot(h, w_ref[...], preferred_element_type=F32)
    q = (proj[:, 0:ATTN_WIDTH] * (HEAD_DIM ** -0.5)).astype(BF16)
    k = proj[:, ATTN_WIDTH:2 * ATTN_WIDTH]
    v = proj[:, 2 * ATTN_WIDTH:3 * ATTN_WIDTH]
    u = proj[:, 3 * ATTN_WIDTH:]

    @pl.when(i >= prompt_tiles)
    def _():
        qs_ref[...] = q
        ks_ref[...] = k
        vs_ref[...] = v
        us_ref[...] = u

    @pl.when(i < prompt_tiles)
    def _():
        q_ref[...] = q
        kb_ref[...] = k.astype(BF16)
        vb_ref[...] = v.astype(BF16)
        kt_ref[...] = k.T
        vt_ref[...] = v.T

        tile_in_seq = lax.rem(i, tiles_per_seq)

        @pl.when(tile_in_seq == 0)
        def _():
            halo_ref[...] = jnp.zeros_like(halo_ref)

        pos = tile_in_seq * TM + lax.broadcasted_iota(jnp.int32, (TM, LANES), 0)
        for g, win in enumerate(POOL_WINDOWS):
            cols = slice(g * POOL_GROUP_WIDTH, (g + 1) * POOL_GROUP_WIDTH)
            ug = u[:, cols]
            s = jnp.concatenate([halo_ref[:, cols], ug], axis=0)
            step = 1
            while step < win:
                s = s + pltpu.roll(s, step, axis=0)
                step *= 2
            cnt = jnp.minimum(pos + 1, win).astype(F32)
            d = (s[HALO:] / cnt - ug).astype(BF16)
            o = jnp.dot(d, pw_ref[g], preferred_element_type=F32)
            ob_ref[:, cols] = (o * ps_ref[:, cols]).astype(BF16)
        halo_ref[...] = u[TM - HALO:, :]

        @pl.when(tile_in_seq == tiles_per_seq - 1)
        def _():
            tail_ref[...] = u[TM - HALO:, :]


def _inproj(x, g, w_in, pool_w, pool_scale, stacks, layer, batch, seq_len):
    t = x.shape[0]
    tp = batch * seq_len
    tiles_per_seq = seq_len // TM
    prompt_tiles = tp // TM
    wide = 3 * ATTN_WIDTH + POOL_WIDTH
    row = lambda i: (i, 0)
    prow = lambda i: (jnp.minimum(i, prompt_tiles - 1), 0)
    srow = lambda i: (jnp.maximum(i - prompt_tiles, 0), 0)

    def trow(i):
        ip = jnp.minimum(i, prompt_tiles - 1)
        return (layer, ip // tiles_per_seq, 0, ip % tiles_per_seq)

    stack_shape = jax.ShapeDtypeStruct(stacks[0].shape, F32)
    return pl.pallas_call(
        functools.partial(_inproj_body, tiles_per_seq=tiles_per_seq, prompt_tiles=prompt_tiles),
        grid=(t // TM,),
        in_specs=[
            pl.BlockSpec((TM, D_MODEL), row),
            _resident((1, D_MODEL)),
            _resident((D_MODEL, wide)),
            _resident((len(POOL_WINDOWS), POOL_GROUP_WIDTH, POOL_GROUP_WIDTH)),
            _resident((1, POOL_WIDTH)),
            pl.BlockSpec(memory_space=pl.ANY),
            pl.BlockSpec(memory_space=pl.ANY),
        ],
        out_specs=[
            pl.BlockSpec((TM, ATTN_WIDTH), prow),
            pl.BlockSpec((TM, ATTN_WIDTH), prow),
            pl.BlockSpec((TM, ATTN_WIDTH), prow),
            pl.BlockSpec((TM, POOL_WIDTH), prow),
            pl.BlockSpec((None, None, ATTN_WIDTH, TM), trow),
            pl.BlockSpec((None, None, ATTN_WIDTH, TM), trow),
            pl.BlockSpec((None, HALO, POOL_WIDTH),
                         lambda i: (jnp.minimum(i, prompt_tiles - 1) // tiles_per_seq, 0, 0)),
            pl.BlockSpec((TM, ATTN_WIDTH), srow),
            pl.BlockSpec((TM, ATTN_WIDTH), srow),
            pl.BlockSpec((TM, ATTN_WIDTH), srow),
            pl.BlockSpec((TM, POOL_WIDTH), srow),
        ],
        out_shape=[
            jax.ShapeDtypeStruct((tp, ATTN_WIDTH), BF16),
            jax.ShapeDtypeStruct((tp, ATTN_WIDTH), BF16),
            jax.ShapeDtypeStruct((tp, ATTN_WIDTH), BF16),
            jax.ShapeDtypeStruct((tp, POOL_WIDTH), BF16),
            stack_shape,
            stack_shape,
            jax.ShapeDtypeStruct((batch, HALO, POOL_WIDTH), F32),
            jax.ShapeDtypeStruct((t - tp, ATTN_WIDTH), BF16),
            jax.ShapeDtypeStruct((t - tp, ATTN_WIDTH), F32),
            jax.ShapeDtypeStruct((t - tp, ATTN_WIDTH), F32),
            jax.ShapeDtypeStruct((t - tp, POOL_WIDTH), F32),
        ],
        input_output_aliases={5: 4, 6: 5},
        scratch_shapes=[pltpu.VMEM((HALO, POOL_WIDTH), F32)],
        compiler_params=pltpu.CompilerParams(
            dimension_semantics=("arbitrary",), vmem_limit_bytes=48 * 1024 * 1024),
        name="inproj",
    )(x, g, w_in, pool_w, pool_scale, *stacks)


def _attn_body(bias_ref, q_ref, k_ref, v_ref, o_ref):
    g = pl.program_id(1)
    i = pl.program_id(2)
    first_q = lax.broadcasted_iota(jnp.int32, (TQ, LANES), 1) < HEAD_DIM
    first_k = lax.broadcasted_iota(jnp.int32, (TK, LANES), 1) < HEAD_DIM
    causal = (lax.broadcasted_iota(jnp.int32, (TQ, TK), 1)
              < lax.broadcasted_iota(jnp.int32, (TQ, TK), 0))
    later = _later_keys(TK)
    qs, biases = [], []
    for p in range(HEAD_PAIRS):
        q = q_ref[:, p * LANES:(p + 1) * LANES]
        zeros_q = jnp.zeros_like(q)
        qs.append((jnp.where(first_q, q, zeros_q), jnp.where(first_q, zeros_q, q)))
        head = 2 * (g * HEAD_PAIRS + p)
        biases.append((bias_ref[head] * LOG2E, bias_ref[head + 1] * LOG2E))

    def tile(j, masked, state):
        rows = pl.ds(pl.multiple_of(j * TK, TK), TK)
        new_state = []
        for p in range(HEAD_PAIRS):
            acc, carries = state[p]
            k = k_ref[rows, p * LANES:(p + 1) * LANES]
            v = v_ref[rows, p * LANES:(p + 1) * LANES]
            zeros_v = jnp.zeros_like(v)
            ws, new_carries = [], []
            for hh in range(2):
                z2 = lax.dot_general(qs[p][hh], k, (((1,), (1,)), ((), ())),
                                     preferred_element_type=F32) * LOG2E + biases[p][hh]
                log_beta, log_1mb = _log2_sigmoids(z2)
                if masked:
                    log_1mb = jnp.where(causal, log_1mb, 0.0)
                stay = _sum_over_later_keys(log_1mb, later) + carries[hh]
                w = jnp.exp2(log_beta + stay)
                if masked:
                    w = jnp.where(causal, w, 0.0)
                ws.append(w.astype(BF16))
                new_carries.append(carries[hh] + jnp.sum(log_1mb, axis=1, keepdims=True))
            v2 = jnp.concatenate([jnp.where(first_k, v, zeros_v), jnp.where(first_k, zeros_v, v)], axis=0)
            acc = acc + jnp.dot(jnp.concatenate(ws, axis=1), v2, preferred_element_type=F32)
            new_state.append((acc, tuple(new_carries)))
        return tuple(new_state)

    zero_carry = jnp.zeros((TQ, 1), F32)
    init = tuple((jnp.zeros((TQ, LANES), F32), (zero_carry, zero_carry)) for _ in range(HEAD_PAIRS))
    state = tile(i, True, init)
    state = lax.fori_loop(0, i, lambda jj, st: tile(i - 1 - jj, False, st), state)
    for p in range(HEAD_PAIRS):
        o_ref[:, p * LANES:(p + 1) * LANES] = state[p][0].astype(BF16)


def _prompt_attn(q, kb, vb, bias, batch, seq_len):
    nq = seq_len // TQ
    width = HEAD_PAIRS * LANES
    return pl.pallas_call(
        _attn_body,
        grid=(batch, ATTN_WIDTH // width, nq),
        in_specs=[
            pl.BlockSpec(memory_space=pltpu.SMEM),
            pl.BlockSpec((TQ, width), lambda b, g, i: (b * nq + i, g)),
            pl.BlockSpec((seq_len, width), lambda b, g, i: (b, g)),
            pl.BlockSpec((seq_len, width), lambda b, g, i: (b, g)),
        ],
        out_specs=pl.BlockSpec((TQ, width), lambda b, g, i: (b * nq + i, g)),
        out_shape=jax.ShapeDtypeStruct((batch * seq_len, ATTN_WIDTH), BF16),
        compiler_params=pltpu.CompilerParams(
            dimension_semantics=("arbitrary", "arbitrary", "arbitrary"),
            vmem_limit_bytes=48 * 1024 * 1024),
        name="prompt_attn",
    )(bias, q, kb, vb)


def _decode_body(pt_ref, q_ref, kn_ref, vn_ref, un_ref, st_ref, bias_ref, pw_ref, ps_ref,
                 ck_hbm, cv_hbm, o_ref, kbuf, vbuf, sem, *, layer, n_pages, page_size):
    b = pl.program_id(0)
    nb = pl.num_programs(0)
    slot = lax.rem(b, 2)
    dec_seq = q_ref.shape[0]
    n_rows = N_HEADS * dec_seq
    pages_per_block = KB // page_size

    def page_copies(seq, sl):
        cps = []
        for p in range(n_pages):
            page = pt_ref[seq, p]
            cps.append(pltpu.make_async_copy(ck_hbm.at[layer, page], kbuf.at[sl, p], sem.at[0, sl]))
            cps.append(pltpu.make_async_copy(cv_hbm.at[layer, page], vbuf.at[sl, p], sem.at[1, sl]))
        return cps

    @pl.when(b == 0)
    def _():
        for cp in page_copies(0, 0):
            cp.start()

    @pl.when(b + 1 < nb)
    def _():
        for cp in page_copies(b + 1, 1 - slot):
            cp.start()

    q = q_ref[...].astype(F32)
    sel = (lax.broadcasted_iota(jnp.int32, (n_rows, dec_seq), 0) % dec_seq
           == lax.broadcasted_iota(jnp.int32, (n_rows, dec_seq), 1)).astype(F32)
    q_rows = jnp.dot(sel, q, preferred_element_type=F32)
    row_head = lax.broadcasted_iota(jnp.int32, (n_rows, ATTN_WIDTH), 0) // dec_seq
    col_head = lax.broadcasted_iota(jnp.int32, (n_rows, ATTN_WIDTH), 1) // HEAD_DIM
    qbd = jnp.where(row_head == col_head, q_rows, 0.0).astype(BF16)
    bias2 = bias_ref[...] * LOG2E

    pad = jnp.zeros((LANES - dec_seq, ATTN_WIDTH), F32)
    kn = jnp.concatenate([kn_ref[...], pad], axis=0).astype(BF16)
    vn = jnp.concatenate([vn_ref[...], pad], axis=0).astype(BF16)
    zn = lax.dot_general(qbd, kn, (((1,), (1,)), ((), ())),
                         preferred_element_type=F32) * LOG2E + bias2[:, :LANES]
    mask_n = (lax.broadcasted_iota(jnp.int32, (n_rows, LANES), 1)
              < lax.broadcasted_iota(jnp.int32, (n_rows, LANES), 0) % dec_seq)
    log_beta_n, log_1mb_n = _log2_sigmoids(zn)
    log_1mb_n = jnp.where(mask_n, log_1mb_n, 0.0)
    stay_n = _sum_over_later_keys(log_1mb_n, _later_keys(LANES))
    w_n = jnp.where(mask_n, jnp.exp2(log_beta_n + stay_n), 0.0).astype(BF16)
    o_new = jnp.dot(w_n, vn, preferred_element_type=F32)
    carry = jnp.sum(log_1mb_n, axis=1, keepdims=True)

    for cp in page_copies(b, slot):
        cp.wait()

    later = _later_keys(KB)
    n_blocks = n_pages // pages_per_block
    ws = [None] * n_blocks
    for jb in reversed(range(n_blocks)):
        pages = range(jb * pages_per_block, (jb + 1) * pages_per_block)
        kblk = jnp.concatenate([kbuf[slot, p] for p in pages], axis=1).astype(BF16)
        z2 = jnp.dot(qbd, kblk, preferred_element_type=F32) * LOG2E + bias2
        log_beta, log_1mb = _log2_sigmoids(z2)
        stay = _sum_over_later_keys(log_1mb, later) + carry
        ws[jb] = jnp.exp2(log_beta + stay).astype(BF16)
        carry = carry + jnp.sum(log_1mb, axis=1, keepdims=True)
    w = jnp.concatenate(ws, axis=1)
    w = jnp.concatenate([w, jnp.zeros((LANES - n_rows, w.shape[1]), BF16)], axis=0)
    vt = jnp.concatenate([vbuf[slot, p] for p in range(n_pages)], axis=1).astype(BF16)
    acc = lax.dot_general(w, vt, (((1,), (1,)), ((), ())),
                          preferred_element_type=F32)[:n_rows] + o_new

    lane = lax.broadcasted_iota(jnp.int32, (dec_seq, LANES), 1)
    for cb in range(ATTN_WIDTH // LANES):
        cols = slice(cb * LANES, (cb + 1) * LANES)
        h0 = 2 * cb
        a0 = acc[h0 * dec_seq:(h0 + 1) * dec_seq, cols]
        a1 = acc[(h0 + 1) * dec_seq:(h0 + 2) * dec_seq, cols]
        o_ref[:, cols] = jnp.where(lane < HEAD_DIM, a0, a1)

    un = un_ref[...]
    ext = jnp.concatenate([st_ref[...], un], axis=0)
    for g, win in enumerate(POOL_WINDOWS):
        cols = slice(g * POOL_GROUP_WIDTH, (g + 1) * POOL_GROUP_WIDTH)
        s = ext[:, cols]
        step = 1
        while step < win:
            s = s + pltpu.roll(s, step, axis=0)
            step *= 2
        d = (s[HALO:] / float(win) - un[:, cols]).astype(BF16).astype(F32)
        o = jnp.dot(d, pw_ref[g], preferred_element_type=F32)
        o_ref[:, ATTN_WIDTH + g * POOL_GROUP_WIDTH:ATTN_WIDTH + (g + 1) * POOL_GROUP_WIDTH] = (
            o * ps_ref[:, cols])


def _decode(page_table, q, kn, vn, un, st, bias_rows, pool_w, pool_scale, cache_kt, cache_vt, layer):
    nb, dec_seq, _ = q.shape
    n_pages = page_table.shape[1]
    page_size = cache_kt.shape[-1]
    n_rows = N_HEADS * dec_seq
    assert KB % page_size == 0 and n_pages % (KB // page_size) == 0 and page_size == LANES
    seq = lambda b, pt: (b, 0, 0)
    const2 = lambda b, pt: (0, 0)
    const3 = lambda b, pt: (0, 0, 0)
    grid_spec = pltpu.PrefetchScalarGridSpec(
        num_scalar_prefetch=1,
        grid=(nb,),
        in_specs=[
            pl.BlockSpec((None, dec_seq, ATTN_WIDTH), seq),
            pl.BlockSpec((None, dec_seq, ATTN_WIDTH), seq),
            pl.BlockSpec((None, dec_seq, ATTN_WIDTH), seq),
            pl.BlockSpec((None, dec_seq, POOL_WIDTH), seq),
            pl.BlockSpec((None, HALO, POOL_WIDTH), seq),
            pl.BlockSpec((n_rows, KB), const2),
            pl.BlockSpec((len(POOL_WINDOWS), POOL_GROUP_WIDTH, POOL_GROUP_WIDTH), const3),
            pl.BlockSpec((1, POOL_WIDTH), const2),
            pl.BlockSpec(memory_space=pl.ANY),
            pl.BlockSpec(memory_space=pl.ANY),
        ],
        out_specs=pl.BlockSpec((None, dec_seq, ATTN_WIDTH + POOL_WIDTH), seq),
        scratch_shapes=[
            pltpu.VMEM((2, n_pages, ATTN_WIDTH, page_size), F32),
            pltpu.VMEM((2, n_pages, ATTN_WIDTH, page_size), F32),
            pltpu.SemaphoreType.DMA((2, 2)),
        ],
    )
    return pl.pallas_call(
        functools.partial(_decode_body, layer=layer, n_pages=n_pages, page_size=page_size),
        grid_spec=grid_spec,
        out_shape=jax.ShapeDtypeStruct((nb, dec_seq, ATTN_WIDTH + POOL_WIDTH), F32),
        compiler_params=pltpu.CompilerParams(
            dimension_semantics=("arbitrary",), vmem_limit_bytes=48 * 1024 * 1024),
        name="sample_attn",
    )(page_table, q, kn, vn, un, st, bias_rows, pool_w, pool_scale, cache_kt, cache_vt)


def _outproj_body(x_ref, oa_ref, ob_ref, dec_ref, w_ref, g_ref, o_ref, *, prompt_tiles):
    prompt_mix = jnp.concatenate([oa_ref[...], ob_ref[...]], axis=1)
    mix = jnp.where(pl.program_id(0) < prompt_tiles, prompt_mix, dec_ref[...].astype(BF16))
    y = jnp.dot(mix, w_ref[...], preferred_element_type=F32)
    o_ref[...] = x_ref[...] + _rms(y, g_ref[...])


def _outproj(x, oa, ob, dec, w_out, post_g):
    t = x.shape[0]
    prompt_tiles = oa.shape[0] // TM
    row = lambda i: (i, 0)
    return pl.pallas_call(
        functools.partial(_outproj_body, prompt_tiles=prompt_tiles),
        grid=(t // TM,),
        in_specs=[
            pl.BlockSpec((TM, D_MODEL), row),
            pl.BlockSpec((TM, ATTN_WIDTH), lambda i: (jnp.minimum(i, prompt_tiles - 1), 0)),
            pl.BlockSpec((TM, POOL_WIDTH), lambda i: (jnp.minimum(i, prompt_tiles - 1), 0)),
            pl.BlockSpec((TM, D_MODEL), lambda i: (jnp.maximum(i - prompt_tiles, 0), 0)),
            _resident((D_MODEL, D_MODEL)),
            _resident((1, D_MODEL)),
        ],
        out_specs=pl.BlockSpec((TM, D_MODEL), row),
        out_shape=jax.ShapeDtypeStruct((t, D_MODEL), F32),
        compiler_params=pltpu.CompilerParams(
            dimension_semantics=("arbitrary",), vmem_limit_bytes=48 * 1024 * 1024),
        name="outproj",
    )(x, oa, ob, dec, w_out, post_g)


def kernel(x_prompt, x_sample, cache_k, cache_v, state_pool, page_table, w_in, w_out, sb_bias, pool_w, pool_scale, ffn1_pre_g, ffn1_post_g, ffn1_w_gate, ffn1_w_up, ffn1_w_down, mix_pre_g, mix_post_g, ffn2_pre_g, ffn2_post_g, ffn2_w_gate, ffn2_w_up, ffn2_w_down):
    batch, seq_len, _ = x_prompt.shape
    dec_batch, dec_seq, _ = x_sample.shape
    depth = w_in.shape[0]
    tp = batch * seq_len
    ts = dec_batch * dec_seq
    assert seq_len % TM == 0 and seq_len % TQ == 0 and ts % TM == 0
    assert N_HEADS * dec_seq <= LANES

    n_pool_pages, page_size = cache_k.shape[1], cache_k.shape[2]
    ckt = cache_k.transpose(0, 1, 3, 4, 2).reshape(depth, n_pool_pages, ATTN_WIDTH, page_size)
    cvt = cache_v.transpose(0, 1, 3, 4, 2).reshape(depth, n_pool_pages, ATTN_WIDTH, page_size)
    st_pad = jnp.pad(state_pool, ((0, 0), (0, 0), (HALO - POOL_BUF, 0), (0, 0)))

    x = jnp.concatenate([x_prompt.reshape(tp, D_MODEL), x_sample.reshape(ts, D_MODEL)], axis=0)
    row = lambda a, l: a[l][None, :]
    per_seq = lambda a: a.reshape(dec_batch, dec_seq, a.shape[-1])
    heads = lambda a: a.reshape(dec_batch, dec_seq, N_HEADS, HEAD_DIM)
    stacks = tuple(jnp.zeros((depth, batch, ATTN_WIDTH, seq_len), F32) for _ in range(2))
    pool_p, k_s, v_s, pool_s = [], [], [], []
    for l in range(depth):
        x = _ffn(x, row(ffn1_pre_g, l), ffn1_w_gate[l].astype(BF16), ffn1_w_up[l].astype(BF16),
                 ffn1_w_down[l].astype(BF16), row(ffn1_post_g, l))

        pw = pool_w[l].astype(BF16)
        ps = row(pool_scale, l)
        q, kb, vb, ob, kt_stack, vt_stack, tail, qs, ks, vs, us = _inproj(
            x, row(mix_pre_g, l), w_in[l].astype(BF16), pw, ps, stacks, l, batch, seq_len)
        stacks = (kt_stack, vt_stack)
        oa = _prompt_attn(q, kb, vb, sb_bias[l], batch, seq_len)

        bias_rows = jnp.broadcast_to(jnp.repeat(sb_bias[l], dec_seq)[:, None], (N_HEADS * dec_seq, KB))
        dec = _decode(page_table, per_seq(qs), per_seq(ks), per_seq(vs), per_seq(us), st_pad[l],
                      bias_rows, pw.astype(F32), ps, ckt, cvt, l)

        x = _outproj(x, oa, ob, dec.reshape(ts, D_MODEL), w_out[l].astype(BF16), row(mix_post_g, l))

        x = _ffn(x, row(ffn2_pre_g, l), ffn2_w_gate[l].astype(BF16), ffn2_w_up[l].astype(BF16),
                 ffn2_w_down[l].astype(BF16), row(ffn2_post_g, l))

        pool_p.append(tail[:, HALO - POOL_BUF:])
        k_s.append(heads(ks))
        v_s.append(heads(vs))
        pool_s.append(jnp.concatenate([state_pool[l], per_seq(us)], axis=1)[:, dec_seq:])

    y_prompt = x[:tp].reshape(batch, seq_len, D_MODEL)
    y_sample = x[tp:].reshape(dec_batch, dec_seq, D_MODEL)
    seq_major = lambda a: a.reshape(depth, batch, N_HEADS, HEAD_DIM, seq_len).transpose(0, 1, 4, 2, 3)
    return (y_prompt, y_sample, seq_major(stacks[0]), seq_major(stacks[1]), jnp.stack(pool_p),
            jnp.stack(k_s), jnp.stack(v_s), jnp.stack(pool_s))
```

```python
import functools

import jax
import jax.numpy as jnp
from jax import lax
from jax.experimental import pallas as pl
from jax.experimental.pallas import tpu as pltpu

D_MODEL = 1024
D_FF = 2816
ATTN_WIDTH = 512
POOL_WIDTH = 512
HEAD_DIM = 64
N_HEADS = 8
POOL_WINDOWS = (2, 4, 8, 16)
POOL_GROUP_WIDTH = 128
POOL_BUF = 15
RMS_EPS = 1e-6
FFN_RES_WEIGHT = 0.5

LANES = 128
HALO = 16
TM = 512
FF_CHUNK = 256
TQ = 256
TK = 256
KB = 256
HEAD_PAIRS = 4

F32 = jnp.float32
BF16 = jnp.bfloat16


def _rms(x, g):
    ms = jnp.mean(x * x, axis=-1, keepdims=True)
    return (x * lax.rsqrt(ms + RMS_EPS)) * g


def _log_sigmoids(z):
    log_beta = jnp.minimum(z, 0.0) - jnp.log(1.0 + jnp.exp(-jnp.abs(z)))
    return log_beta, log_beta - z


def _later_keys(n):
    later = (lax.broadcasted_iota(jnp.int32, (n, n), 0)
             > lax.broadcasted_iota(jnp.int32, (n, n), 1)).astype(BF16)
    return jnp.concatenate([later, later], axis=0)


def _sum_over_later_keys(log_1mb, later2):
    hi = log_1mb.astype(BF16)
    lo = (log_1mb - hi.astype(F32)).astype(BF16)
    return jnp.dot(jnp.concatenate([hi, lo], axis=1), later2, preferred_element_type=F32)


def _prompt_rows(prompt_tiles, width):
    return pl.BlockSpec((TM, width), lambda i: (jnp.minimum(i, prompt_tiles - 1), 0))


def _sample_rows(prompt_tiles, width):
    return pl.BlockSpec((TM, width), lambda i: (jnp.maximum(i - prompt_tiles, 0), 0))


def _per_row_kind(prompt_tiles, tile_fn, prompt_refs, sample_refs):
    @pl.when(pl.program_id(0) < prompt_tiles)
    def _():
        tile_fn(*prompt_refs)

    @pl.when(pl.program_id(0) >= prompt_tiles)
    def _():
        tile_fn(*sample_refs)


def _row_pair_shapes(xp, xs):
    return [jax.ShapeDtypeStruct(xp.shape, xp.dtype), jax.ShapeDtypeStruct(xs.shape, xs.dtype)]


def _ffn_body(xp_ref, xs_ref, pre_ref, wg_ref, wu_ref, wd_ref, post_ref, yp_ref, ys_ref, a_ref,
              *, prompt_tiles):
    def tile(x_ref, y_ref):
        x = x_ref[...]
        h = _rms(x, pre_ref[...]).astype(BF16)
        for c in range(D_FF // FF_CHUNK):
            cols = pl.ds(c * FF_CHUNK, FF_CHUNK)
            g = jnp.dot(h, wg_ref[:, cols], preferred_element_type=F32)
            u = jnp.dot(h, wu_ref[:, cols], preferred_element_type=F32)
            a_ref[:, cols] = (g / (1.0 + jnp.exp(-g)) * u).astype(BF16)
        y = jnp.dot(a_ref[...], wd_ref[...], preferred_element_type=F32)
        y_ref[...] = x + FFN_RES_WEIGHT * _rms(y, post_ref[...])

    _per_row_kind(prompt_tiles, tile, (xp_ref, yp_ref), (xs_ref, ys_ref))


def _resident(shape):
    return pl.BlockSpec(shape, lambda i: (0,) * len(shape), pipeline_mode=pl.Buffered(1))


def _ffn(xp, xs, pre_g, wg, wu, wd, post_g):
    prompt_tiles = xp.shape[0] // TM
    return pl.pallas_call(
        functools.partial(_ffn_body, prompt_tiles=prompt_tiles),
        grid=(prompt_tiles + xs.shape[0] // TM,),
        in_specs=[
            _prompt_rows(prompt_tiles, D_MODEL),
            _sample_rows(prompt_tiles, D_MODEL),
            _resident((1, D_MODEL)),
            _resident((D_MODEL, D_FF)),
            _resident((D_MODEL, D_FF)),
            _resident((D_FF, D_MODEL)),
            _resident((1, D_MODEL)),
        ],
        out_specs=[_prompt_rows(prompt_tiles, D_MODEL), _sample_rows(prompt_tiles, D_MODEL)],
        out_shape=_row_pair_shapes(xp, xs),
        scratch_shapes=[pltpu.VMEM((TM, D_FF), BF16)],
        compiler_params=pltpu.CompilerParams(
            dimension_semantics=("arbitrary",), vmem_limit_bytes=48 * 1024 * 1024),
        name="ffn",
    )(xp, xs, pre_g, wg, wu, wd, post_g)


def _inproj_body(xp_ref, xs_ref, g_ref, w_ref, pw_ref, ps_ref, kt_in_ref, vt_in_ref,
                 q_ref, kb_ref, vb_ref, ob_ref, kt_ref, vt_ref, tail_ref,
                 qs_ref, ks_ref, vs_ref, us_ref, halo_ref, *, tiles_per_seq, prompt_tiles):
    del kt_in_ref, vt_in_ref
    i = pl.program_id(0)

    def project(x_ref):
        h = _rms(x_ref[...], g_ref[...]).astype(BF16)
        proj = jnp.dot(h, w_ref[...], preferred_element_type=F32)
        q = (proj[:, 0:ATTN_WIDTH] * (HEAD_DIM ** -0.5)).astype(BF16)
        return (q, proj[:, ATTN_WIDTH:2 * ATTN_WIDTH], proj[:, 2 * ATTN_WIDTH:3 * ATTN_WIDTH],
                proj[:, 3 * ATTN_WIDTH:])

    @pl.when(i >= prompt_tiles)
    def _():
        qs_ref[...], ks_ref[...], vs_ref[...], us_ref[...] = project(xs_ref)

    @pl.when(i < prompt_tiles)
    def _():
        q, k, v, u = project(xp_ref)
        q_ref[...] = q
        kb_ref[...] = k.astype(BF16)
        vb_ref[...] = v.astype(BF16)
        kt_ref[...] = k.T
        vt_ref[...] = v.T

        tile_in_seq = lax.rem(i, tiles_per_seq)

        @pl.when(tile_in_seq == 0)
        def _():
            halo_ref[...] = jnp.zeros_like(halo_ref)

        pos = tile_in_seq * TM + lax.broadcasted_iota(jnp.int32, (TM, LANES), 0)
        for g, win in enumerate(POOL_WINDOWS):
            cols = slice(g * POOL_GROUP_WIDTH, (g + 1) * POOL_GROUP_WIDTH)
            ug = u[:, cols]
            s = jnp.concatenate([halo_ref[:, cols], ug], axis=0)
            step = 1
            while step < win:
                s = s + pltpu.roll(s, step, axis=0)
                step *= 2
            cnt = jnp.minimum(pos + 1, win).astype(F32)
            d = (s[HALO:] / cnt - ug).astype(BF16)
            o = jnp.dot(d, pw_ref[g], preferred_element_type=F32)
            ob_ref[:, cols] = (o * ps_ref[:, cols]).astype(BF16)
        halo_ref[...] = u[TM - HALO:, :]

        @pl.when(tile_in_seq == tiles_per_seq - 1)
        def _():
            tail_ref[...] = u[TM - HALO:, :]


def _inproj(xp, xs, g, w_in, pool_w, pool_scale, stacks, layer, batch, seq_len):
    tp, ts = xp.shape[0], xs.shape[0]
    tiles_per_seq = seq_len // TM
    prompt_tiles = tp // TM
    wide = 3 * ATTN_WIDTH + POOL_WIDTH

    def trow(i):
        ip = jnp.minimum(i, prompt_tiles - 1)
        return (layer, ip // tiles_per_seq, 0, ip % tiles_per_seq)

    stack_shape = jax.ShapeDtypeStruct(stacks[0].shape, F32)
    return pl.pallas_call(
        functools.partial(_inproj_body, tiles_per_seq=tiles_per_seq, prompt_tiles=prompt_tiles),
        grid=(prompt_tiles + ts // TM,),
        in_specs=[
            _prompt_rows(prompt_tiles, D_MODEL),
            _sample_rows(prompt_tiles, D_MODEL),
            _resident((1, D_MODEL)),
            _resident((D_MODEL, wide)),
            _resident((len(POOL_WINDOWS), POOL_GROUP_WIDTH, POOL_GROUP_WIDTH)),
            _resident((1, POOL_WIDTH)),
            pl.BlockSpec(memory_space=pl.ANY),
            pl.BlockSpec(memory_space=pl.ANY),
        ],
        out_specs=[
            _prompt_rows(prompt_tiles, ATTN_WIDTH),
            _prompt_rows(prompt_tiles, ATTN_WIDTH),
            _prompt_rows(prompt_tiles, ATTN_WIDTH),
            _prompt_rows(prompt_tiles, POOL_WIDTH),
            pl.BlockSpec((None, None, ATTN_WIDTH, TM), trow),
            pl.BlockSpec((None, None, ATTN_WIDTH, TM), trow),
            pl.BlockSpec((None, HALO, POOL_WIDTH),
                         lambda i: (jnp.minimum(i, prompt_tiles - 1) // tiles_per_seq, 0, 0)),
            _sample_rows(prompt_tiles, ATTN_WIDTH),
            _sample_rows(prompt_tiles, ATTN_WIDTH),
            _sample_rows(prompt_tiles, ATTN_WIDTH),
            _sample_rows(prompt_tiles, POOL_WIDTH),
        ],
        out_shape=[
            jax.ShapeDtypeStruct((tp, ATTN_WIDTH), BF16),
            jax.ShapeDtypeStruct((tp, ATTN_WIDTH), BF16),
            jax.ShapeDtypeStruct((tp, ATTN_WIDTH), BF16),
            jax.ShapeDtypeStruct((tp, POOL_WIDTH), BF16),
            stack_shape,
            stack_shape,
            jax.ShapeDtypeStruct((batch, HALO, POOL_WIDTH), F32),
            jax.ShapeDtypeStruct((ts, ATTN_WIDTH), BF16),
            jax.ShapeDtypeStruct((ts, ATTN_WIDTH), F32),
            jax.ShapeDtypeStruct((ts, ATTN_WIDTH), F32),
            jax.ShapeDtypeStruct((ts, POOL_WIDTH), F32),
        ],
        input_output_aliases={6: 4, 7: 5},
        scratch_shapes=[pltpu.VMEM((HALO, POOL_WIDTH), F32)],
        compiler_params=pltpu.CompilerParams(
            dimension_semantics=("arbitrary",), vmem_limit_bytes=48 * 1024 * 1024),
        name="inproj",
    )(xp, xs, g, w_in, pool_w, pool_scale, *stacks)


def _attn_body(bias_ref, q_ref, k_ref, v_ref, o_ref):
    g = pl.program_id(1)
    i = pl.program_id(2)
    first_q = lax.broadcasted_iota(jnp.int32, (TQ, LANES), 1) < HEAD_DIM
    first_k = lax.broadcasted_iota(jnp.int32, (TK, LANES), 1) < HEAD_DIM
    causal = (lax.broadcasted_iota(jnp.int32, (TQ, TK), 1)
              < lax.broadcasted_iota(jnp.int32, (TQ, TK), 0))
    later = _later_keys(TK)
    qs, biases = [], []
    for p in range(HEAD_PAIRS):
        q = q_ref[:, p * LANES:(p + 1) * LANES]
        zeros_q = jnp.zeros_like(q)
        qs.append((jnp.where(first_q, q, zeros_q), jnp.where(first_q, zeros_q, q)))
        head = 2 * (g * HEAD_PAIRS + p)
        biases.append((bias_ref[head], bias_ref[head + 1]))

    def tile(j, masked, state):
        rows = pl.ds(pl.multiple_of(j * TK, TK), TK)
        new_state = []
        for p in range(HEAD_PAIRS):
            acc, carries = state[p]
            k = k_ref[rows, p * LANES:(p + 1) * LANES]
            v = v_ref[rows, p * LANES:(p + 1) * LANES]
            zeros_v = jnp.zeros_like(v)
            ws, new_carries = [], []
            for hh in range(2):
                z = lax.dot_general(qs[p][hh], k, (((1,), (1,)), ((), ())),
                                    preferred_element_type=F32) + biases[p][hh]
                log_beta, log_1mb = _log_sigmoids(z)
                if masked:
                    log_1mb = jnp.where(causal, log_1mb, 0.0)
                stay = _sum_over_later_keys(log_1mb, later) + carries[hh]
                w = jnp.exp(log_beta + stay)
                if masked:
                    w = jnp.where(causal, w, 0.0)
                ws.append(w.astype(BF16))
                new_carries.append(carries[hh] + jnp.sum(log_1mb, axis=1, keepdims=True))
            v2 = jnp.concatenate([jnp.where(first_k, v, zeros_v), jnp.where(first_k, zeros_v, v)], axis=0)
            acc = acc + jnp.dot(jnp.concatenate(ws, axis=1), v2, preferred_element_type=F32)
            new_state.append((acc, tuple(new_carries)))
        return tuple(new_state)

    zero_carry = jnp.zeros((TQ, 1), F32)
    init = tuple((jnp.zeros((TQ, LANES), F32), (zero_carry, zero_carry)) for _ in range(HEAD_PAIRS))
    state = tile(i, True, init)
    state = lax.fori_loop(0, i, lambda jj, st: tile(i - 1 - jj, False, st), state)
    for p in range(HEAD_PAIRS):
        o_ref[:, p * LANES:(p + 1) * LANES] = state[p][0].astype(BF16)


def _prompt_attn(q, kb, vb, bias, batch, seq_len):
    nq = seq_len // TQ
    width = HEAD_PAIRS * LANES
    return pl.pallas_call(
        _attn_body,
        grid=(batch, ATTN_WIDTH // width, nq),
        in_specs=[
            pl.BlockSpec(memory_space=pltpu.SMEM),
            pl.BlockSpec((TQ, width), lambda b, g, i: (b * nq + i, g)),
            pl.BlockSpec((seq_len, width), lambda b, g, i: (b, g)),
            pl.BlockSpec((seq_len, width), lambda b, g, i: (b, g)),
        ],
        out_specs=pl.BlockSpec((TQ, width), lambda b, g, i: (b * nq + i, g)),
        out_shape=jax.ShapeDtypeStruct((batch * seq_len, ATTN_WIDTH), BF16),
        compiler_params=pltpu.CompilerParams(
            dimension_semantics=("arbitrary", "arbitrary", "arbitrary"),
            vmem_limit_bytes=48 * 1024 * 1024),
        name="prompt_attn",
    )(bias, q, kb, vb)


def _decode_body(pt_ref, q_ref, kn_ref, vn_ref, un_ref, st_ref, bias_ref, pw_ref, ps_ref,
                 ck_hbm, cv_hbm, o_ref, kbuf, vbuf, sem, *, layer, n_pages, page_size):
    b = pl.program_id(0)
    nb = pl.num_programs(0)
    slot = lax.rem(b, 2)
    dec_seq = q_ref.shape[0]
    n_rows = N_HEADS * dec_seq
    pages_per_block = KB // page_size

    def page_copies(seq, sl):
        cps = []
        for p in range(n_pages):
            page = pt_ref[seq, p]
            cps.append(pltpu.make_async_copy(ck_hbm.at[layer, page], kbuf.at[sl, p], sem.at[0, sl]))
            cps.append(pltpu.make_async_copy(cv_hbm.at[layer, page], vbuf.at[sl, p], sem.at[1, sl]))
        return cps

    @pl.when(b == 0)
    def _():
        for cp in page_copies(0, 0):
            cp.start()

    @pl.when(b + 1 < nb)
    def _():
        for cp in page_copies(b + 1, 1 - slot):
            cp.start()

    q = q_ref[...].astype(F32)
    sel = (lax.broadcasted_iota(jnp.int32, (n_rows, dec_seq), 0) % dec_seq
           == lax.broadcasted_iota(jnp.int32, (n_rows, dec_seq), 1)).astype(F32)
    q_rows = jnp.dot(sel, q, preferred_element_type=F32)
    row_head = lax.broadcasted_iota(jnp.int32, (n_rows, ATTN_WIDTH), 0) // dec_seq
    col_head = lax.broadcasted_iota(jnp.int32, (n_rows, ATTN_WIDTH), 1) // HEAD_DIM
    qbd = jnp.where(row_head == col_head, q_rows, 0.0).astype(BF16)
    bias = bias_ref[...]

    pad = jnp.zeros((LANES - dec_seq, ATTN_WIDTH), F32)
    kn = jnp.concatenate([kn_ref[...], pad], axis=0).astype(BF16)
    vn = jnp.concatenate([vn_ref[...], pad], axis=0).astype(BF16)
    zn = lax.dot_general(qbd, kn, (((1,), (1,)), ((), ())),
                         preferred_element_type=F32) + bias[:, :LANES]
    mask_n = (lax.broadcasted_iota(jnp.int32, (n_rows, LANES), 1)
              < lax.broadcasted_iota(jnp.int32, (n_rows, LANES), 0) % dec_seq)
    log_beta_n, log_1mb_n = _log_sigmoids(zn)
    log_1mb_n = jnp.where(mask_n, log_1mb_n, 0.0)
    stay_n = _sum_over_later_keys(log_1mb_n, _later_keys(LANES))
    w_n = jnp.where(mask_n, jnp.exp(log_beta_n + stay_n), 0.0).astype(BF16)
    o_new = jnp.dot(w_n, vn, preferred_element_type=F32)
    carry = jnp.sum(log_1mb_n, axis=1, keepdims=True)

    for cp in page_copies(b, slot):
        cp.wait()

    later = _later_keys(KB)
    n_blocks = n_pages // pages_per_block
    ws = [None] * n_blocks
    for jb in reversed(range(n_blocks)):
        pages = range(jb * pages_per_block, (jb + 1) * pages_per_block)
        kblk = jnp.concatenate([kbuf[slot, p] for p in pages], axis=1).astype(BF16)
        z = jnp.dot(qbd, kblk, preferred_element_type=F32) + bias
        log_beta, log_1mb = _log_sigmoids(z)
        stay = _sum_over_later_keys(log_1mb, later) + carry
        ws[jb] = jnp.exp(log_beta + stay).astype(BF16)
        carry = carry + jnp.sum(log_1mb, axis=1, keepdims=True)
    w = jnp.concatenate(ws, axis=1)
    w = jnp.concatenate([w, jnp.zeros((LANES - n_rows, w.shape[1]), BF16)], axis=0)
    vt = jnp.concatenate([vbuf[slot, p] for p in range(n_pages)], axis=1).astype(BF16)
    acc = lax.dot_general(w, vt, (((1,), (1,)), ((), ())),
                          preferred_element_type=F32)[:n_rows] + o_new

    lane = lax.broadcasted_iota(jnp.int32, (dec_seq, LANES), 1)
    for cb in range(ATTN_WIDTH // LANES):
        cols = slice(cb * LANES, (cb + 1) * LANES)
        h0 = 2 * cb
        a0 = acc[h0 * dec_seq:(h0 + 1) * dec_seq, cols]
        a1 = acc[(h0 + 1) * dec_seq:(h0 + 2) * dec_seq, cols]
        o_ref[:, cols] = jnp.where(lane < HEAD_DIM, a0, a1)

    un = un_ref[...]
    ext = jnp.concatenate([st_ref[...], un], axis=0)
    for g, win in enumerate(POOL_WINDOWS):
        cols = slice(g * POOL_GROUP_WIDTH, (g + 1) * POOL_GROUP_WIDTH)
        s = ext[:, cols]
        step = 1
        while step < win:
            s = s + pltpu.roll(s, step, axis=0)
            step *= 2
        d = (s[HALO:] / float(win) - un[:, cols]).astype(BF16).astype(F32)
        o = jnp.dot(d, pw_ref[g], preferred_element_type=F32)
        o_ref[:, ATTN_WIDTH + g * POOL_GROUP_WIDTH:ATTN_WIDTH + (g + 1) * POOL_GROUP_WIDTH] = (
            o * ps_ref[:, cols])


def _decode(page_table, q, kn, vn, un, st, bias_rows, pool_w, pool_scale, cache_kt, cache_vt, layer):
    nb, dec_seq, _ = q.shape
    n_pages = page_table.shape[1]
    page_size = cache_kt.shape[-1]
    n_rows = N_HEADS * dec_seq
    assert KB % page_size == 0 and n_pages % (KB // page_size) == 0 and page_size == LANES
    seq = lambda b, pt: (b, 0, 0)
    const2 = lambda b, pt: (0, 0)
    const3 = lambda b, pt: (0, 0, 0)
    grid_spec = pltpu.PrefetchScalarGridSpec(
        num_scalar_prefetch=1,
        grid=(nb,),
        in_specs=[
            pl.BlockSpec((None, dec_seq, ATTN_WIDTH), seq),
            pl.BlockSpec((None, dec_seq, ATTN_WIDTH), seq),
            pl.BlockSpec((None, dec_seq, ATTN_WIDTH), seq),
            pl.BlockSpec((None, dec_seq, POOL_WIDTH), seq),
            pl.BlockSpec((None, HALO, POOL_WIDTH), seq),
            pl.BlockSpec((n_rows, KB), const2),
            pl.BlockSpec((len(POOL_WINDOWS), POOL_GROUP_WIDTH, POOL_GROUP_WIDTH), const3),
            pl.BlockSpec((1, POOL_WIDTH), const2),
            pl.BlockSpec(memory_space=pl.ANY),
            pl.BlockSpec(memory_space=pl.ANY),
        ],
        out_specs=pl.BlockSpec((None, dec_seq, ATTN_WIDTH + POOL_WIDTH), seq),
        scratch_shapes=[
            pltpu.VMEM((2, n_pages, ATTN_WIDTH, page_size), F32),
            pltpu.VMEM((2, n_pages, ATTN_WIDTH, page_size), F32),
            pltpu.SemaphoreType.DMA((2, 2)),
        ],
    )
    return pl.pallas_call(
        functools.partial(_decode_body, layer=layer, n_pages=n_pages, page_size=page_size),
        grid_spec=grid_spec,
        out_shape=jax.ShapeDtypeStruct((nb, dec_seq, ATTN_WIDTH + POOL_WIDTH), F32),
        compiler_params=pltpu.CompilerParams(
            dimension_semantics=("arbitrary",), vmem_limit_bytes=48 * 1024 * 1024),
        name="sample_attn",
    )(page_table, q, kn, vn, un, st, bias_rows, pool_w, pool_scale, cache_kt, cache_vt)


def _outproj_body(xp_ref, xs_ref, oa_ref, ob_ref, dec_ref, w_ref, g_ref, yp_ref, ys_ref, *, prompt_tiles):
    def tile(mix_refs, x_ref, y_ref):
        mix = jnp.concatenate([r[...].astype(BF16) for r in mix_refs], axis=1)
        y = jnp.dot(mix, w_ref[...], preferred_element_type=F32)
        y_ref[...] = x_ref[...] + _rms(y, g_ref[...])

    _per_row_kind(prompt_tiles, tile, ((oa_ref, ob_ref), xp_ref, yp_ref), ((dec_ref,), xs_ref, ys_ref))


def _outproj(xp, xs, oa, ob, dec, w_out, post_g):
    prompt_tiles = xp.shape[0] // TM
    return pl.pallas_call(
        functools.partial(_outproj_body, prompt_tiles=prompt_tiles),
        grid=(prompt_tiles + xs.shape[0] // TM,),
        in_specs=[
            _prompt_rows(prompt_tiles, D_MODEL),
            _sample_rows(prompt_tiles, D_MODEL),
            _prompt_rows(prompt_tiles, ATTN_WIDTH),
            _prompt_rows(prompt_tiles, POOL_WIDTH),
            _sample_rows(prompt_tiles, D_MODEL),
            _resident((D_MODEL, D_MODEL)),
            _resident((1, D_MODEL)),
        ],
        out_specs=[_prompt_rows(prompt_tiles, D_MODEL), _sample_rows(prompt_tiles, D_MODEL)],
        out_shape=_row_pair_shapes(xp, xs),
        compiler_params=pltpu.CompilerParams(
            dimension_semantics=("arbitrary",), vmem_limit_bytes=48 * 1024 * 1024),
        name="outproj",
    )(xp, xs, oa, ob, dec, w_out, post_g)


def kernel(x_prompt, x_sample, cache_k, cache_v, state_pool, page_table, w_in, w_out, sb_bias, pool_w, pool_scale, ffn1_pre_g, ffn1_post_g, ffn1_w_gate, ffn1_w_up, ffn1_w_down, mix_pre_g, mix_post_g, ffn2_pre_g, ffn2_post_g, ffn2_w_gate, ffn2_w_up, ffn2_w_down):
    batch, seq_len, _ = x_prompt.shape
    dec_batch, dec_seq, _ = x_sample.shape
    depth = w_in.shape[0]
    tp = batch * seq_len
    ts = dec_batch * dec_seq
    assert seq_len % TM == 0 and seq_len % TQ == 0 and ts % TM == 0
    assert N_HEADS * dec_seq <= LANES

    n_pool_pages, page_size = cache_k.shape[1], cache_k.shape[2]
    ckt = cache_k.transpose(0, 1, 3, 4, 2).reshape(depth, n_pool_pages, ATTN_WIDTH, page_size)
    cvt = cache_v.transpose(0, 1, 3, 4, 2).reshape(depth, n_pool_pages, ATTN_WIDTH, page_size)
    st_pad = jnp.pad(state_pool, ((0, 0), (0, 0), (HALO - POOL_BUF, 0), (0, 0)))

    xp, xs = x_prompt.reshape(tp, D_MODEL), x_sample.reshape(ts, D_MODEL)
    row = lambda a, l: a[l][None, :]
    per_seq = lambda a: a.reshape(dec_batch, dec_seq, a.shape[-1])
    heads = lambda a: a.reshape(dec_batch, dec_seq, N_HEADS, HEAD_DIM)
    stacks = tuple(jnp.zeros((depth, batch, ATTN_WIDTH, seq_len), F32) for _ in range(2))
    pool_p, k_s, v_s, pool_s = [], [], [], []
    for l in range(depth):
        xp, xs = _ffn(xp, xs, row(ffn1_pre_g, l), ffn1_w_gate[l].astype(BF16), ffn1_w_up[l].astype(BF16),
                      ffn1_w_down[l].astype(BF16), row(ffn1_post_g, l))

        pw = pool_w[l].astype(BF16)
        ps = row(pool_scale, l)
        q, kb, vb, ob, kt_stack, vt_stack, tail, qs, ks, vs, us = _inproj(
            xp, xs, row(mix_pre_g, l), w_in[l].astype(BF16), pw, ps, stacks, l, batch, seq_len)
        stacks = (kt_stack, vt_stack)
        oa = _prompt_attn(q, kb, vb, sb_bias[l], batch, seq_len)

        bias_rows = jnp.broadcast_to(jnp.repeat(sb_bias[l], dec_seq)[:, None], (N_HEADS * dec_seq, KB))
        dec = _decode(page_table, per_seq(qs), per_seq(ks), per_seq(vs), per_seq(us), st_pad[l],
                      bias_rows, pw.astype(F32), ps, ckt, cvt, l)

        xp, xs = _outproj(xp, xs, oa, ob, dec.reshape(ts, D_MODEL), w_out[l].astype(BF16),
                          row(mix_post_g, l))

        xp, xs = _ffn(xp, xs, row(ffn2_pre_g, l), ffn2_w_gate[l].astype(BF16), ffn2_w_up[l].astype(BF16),
                      ffn2_w_down[l].astype(BF16), row(ffn2_post_g, l))

        pool_p.append(tail[:, HALO - POOL_BUF:])
        k_s.append(heads(ks))
        v_s.append(heads(vs))
        pool_s.append(jnp.concatenate([state_pool[l], per_seq(us)], axis=1)[:, dec_seq:])

    y_prompt = xp.reshape(batch, seq_len, D_MODEL)
    y_sample = xs.reshape(dec_batch, dec_seq, D_MODEL)
    seq_major = lambda a: a.reshape(depth, batch, N_HEADS, HEAD_DIM, seq_len).transpose(0, 1, 4, 2, 3)
    return (y_prompt, y_sample, seq_major(stacks[0]), seq_major(stacks[1]), jnp.stack(pool_p),
            jnp.stack(k_s), jnp.stack(v_s), jnp.stack(pool_s))
```

```python
import functools

import jax
import jax.numpy as jnp
from jax import lax
from jax.experimental import pallas as pl
from jax.experimental.pallas import tpu as pltpu

D_MODEL = 1024
D_FF = 2816
ATTN_WIDTH = 512
POOL_WIDTH = 512
HEAD_DIM = 64
N_HEADS = 8
POOL_WINDOWS = (2, 4, 8, 16)
POOL_GROUP_WIDTH = 128
POOL_BUF = 15
RMS_EPS = 1e-6
FFN_RES_WEIGHT = 0.5

LANES = 128
HALO = 16
TM = 512
FF_CHUNK = 256
TQ = 256
TK = 256
KB = 256
SEQS_PER_STEP = 2
HEAD_PAIRS = 4

F32 = jnp.float32
BF16 = jnp.bfloat16


def _rms(x, g):
    ms = jnp.mean(x * x, axis=-1, keepdims=True)
    return (x * lax.rsqrt(ms + RMS_EPS)) * g


def _log_sigmoids(z):
    log_beta = jnp.minimum(z, 0.0) - jnp.log(1.0 + jnp.exp(-jnp.abs(z)))
    return log_beta, log_beta - z


def _later_keys(n):
    later = (lax.broadcasted_iota(jnp.int32, (n, n), 0)
             > lax.broadcasted_iota(jnp.int32, (n, n), 1)).astype(BF16)
    return jnp.concatenate([later, later], axis=0)


def _sum_over_later_keys(log_1mb, later2):
    hi = log_1mb.astype(BF16)
    lo = (log_1mb - hi.astype(F32)).astype(BF16)
    return jnp.dot(jnp.concatenate([hi, lo], axis=1), later2, preferred_element_type=F32)


def _prompt_rows(prompt_tiles, width):
    return pl.BlockSpec((TM, width), lambda i: (jnp.minimum(i, prompt_tiles - 1), 0))


def _sample_rows(prompt_tiles, width):
    return pl.BlockSpec((TM, width), lambda i: (jnp.maximum(i - prompt_tiles, 0), 0))


def _per_row_kind(prompt_tiles, tile_fn, prompt_refs, sample_refs):
    @pl.when(pl.program_id(0) < prompt_tiles)
    def _():
        tile_fn(*prompt_refs)

    @pl.when(pl.program_id(0) >= prompt_tiles)
    def _():
        tile_fn(*sample_refs)


def _row_pair_shapes(xp, xs):
    return [jax.ShapeDtypeStruct(xp.shape, xp.dtype), jax.ShapeDtypeStruct(xs.shape, xs.dtype)]


def _ffn_body(xp_ref, xs_ref, pre_ref, wg_ref, wu_ref, wd_ref, post_ref, yp_ref, ys_ref, a_ref,
              *, prompt_tiles):
    def tile(x_ref, y_ref):
        x = x_ref[...]
        h = _rms(x, pre_ref[...]).astype(BF16)
        for c in range(D_FF // FF_CHUNK):
            cols = pl.ds(c * FF_CHUNK, FF_CHUNK)
            g = jnp.dot(h, wg_ref[:, cols], preferred_element_type=F32)
            u = jnp.dot(h, wu_ref[:, cols], preferred_element_type=F32)
            a_ref[:, cols] = (g / (1.0 + jnp.exp(-g)) * u).astype(BF16)
        y = jnp.dot(a_ref[...], wd_ref[...], preferred_element_type=F32)
        y_ref[...] = x + FFN_RES_WEIGHT * _rms(y, post_ref[...])

    _per_row_kind(prompt_tiles, tile, (xp_ref, yp_ref), (xs_ref, ys_ref))


def _resident(shape):
    return pl.BlockSpec(shape, lambda i: (0,) * len(shape), pipeline_mode=pl.Buffered(1))


def _layer_resident(layer, shape):
    return pl.BlockSpec((None,) + shape, lambda i: (layer,) + (0,) * len(shape),
                        pipeline_mode=pl.Buffered(1))


def _ffn(xp, xs, pre_g, wg, wu, wd, post_g, layer):
    prompt_tiles = xp.shape[0] // TM
    return pl.pallas_call(
        functools.partial(_ffn_body, prompt_tiles=prompt_tiles),
        grid=(prompt_tiles + xs.shape[0] // TM,),
        in_specs=[
            _prompt_rows(prompt_tiles, D_MODEL),
            _sample_rows(prompt_tiles, D_MODEL),
            _resident((1, D_MODEL)),
            _layer_resident(layer, (D_MODEL, D_FF)),
            _layer_resident(layer, (D_MODEL, D_FF)),
            _layer_resident(layer, (D_FF, D_MODEL)),
            _resident((1, D_MODEL)),
        ],
        out_specs=[_prompt_rows(prompt_tiles, D_MODEL), _sample_rows(prompt_tiles, D_MODEL)],
        out_shape=_row_pair_shapes(xp, xs),
        scratch_shapes=[pltpu.VMEM((TM, D_FF), BF16)],
        compiler_params=pltpu.CompilerParams(
            dimension_semantics=("arbitrary",), vmem_limit_bytes=48 * 1024 * 1024),
        name="ffn",
    )(xp, xs, pre_g, wg, wu, wd, post_g)


def _inproj_body(xp_ref, xs_ref, g_ref, w_ref, pw_ref, ps_ref, kt_in_ref, vt_in_ref,
                 q_ref, kb_ref, vb_ref, ob_ref, kt_ref, vt_ref, tail_ref,
                 qs_ref, ks_ref, vs_ref, us_ref, halo_ref, *, tiles_per_seq, prompt_tiles):
    del kt_in_ref, vt_in_ref
    i = pl.program_id(0)

    def project(x_ref):
        h = _rms(x_ref[...], g_ref[...]).astype(BF16)
        proj = jnp.dot(h, w_ref[...], preferred_element_type=F32)
        q = (proj[:, 0:ATTN_WIDTH] * (HEAD_DIM ** -0.5)).astype(BF16)
        return (q, proj[:, ATTN_WIDTH:2 * ATTN_WIDTH], proj[:, 2 * ATTN_WIDTH:3 * ATTN_WIDTH],
                proj[:, 3 * ATTN_WIDTH:])

    @pl.when(i >= prompt_tiles)
    def _():
        qs_ref[...], ks_ref[...], vs_ref[...], us_ref[...] = project(xs_ref)

    @pl.when(i < prompt_tiles)
    def _():
        q, k, v, u = project(xp_ref)
        q_ref[...] = q
        kb_ref[...] = k.astype(BF16)
        vb_ref[...] = v.astype(BF16)
        kt_ref[...] = k.T
        vt_ref[...] = v.T

        tile_in_seq = lax.rem(i, tiles_per_seq)

        @pl.when(tile_in_seq == 0)
        def _():
            halo_ref[...] = jnp.zeros_like(halo_ref)

        pos = tile_in_seq * TM + lax.broadcasted_iota(jnp.int32, (TM, LANES), 0)
        for g, win in enumerate(POOL_WINDOWS):
            cols = slice(g * POOL_GROUP_WIDTH, (g + 1) * POOL_GROUP_WIDTH)
            ug = u[:, cols]
            s = jnp.concatenate([halo_ref[:, cols], ug], axis=0)
            step = 1
            while step < win:
                s = s + pltpu.roll(s, step, axis=0)
                step *= 2
            cnt = jnp.minimum(pos + 1, win).astype(F32)
            d = (s[HALO:] / cnt - ug).astype(BF16)
            o = jnp.dot(d, pw_ref[g], preferred_element_type=F32)
            ob_ref[:, cols] = (o * ps_ref[:, cols]).astype(BF16)
        halo_ref[...] = u[TM - HALO:, :]

        @pl.when(tile_in_seq == tiles_per_seq - 1)
        def _():
            tail_ref[...] = u[TM - HALO:, :]


def _inproj(xp, xs, g, w_in, pool_w, pool_scale, stacks, layer, batch, seq_len):
    tp, ts = xp.shape[0], xs.shape[0]
    tiles_per_seq = seq_len // TM
    prompt_tiles = tp // TM
    wide = 3 * ATTN_WIDTH + POOL_WIDTH

    def trow(i):
        ip = jnp.minimum(i, prompt_tiles - 1)
        return (layer, ip // tiles_per_seq, 0, ip % tiles_per_seq)

    stack_shape = jax.ShapeDtypeStruct(stacks[0].shape, F32)
    return pl.pallas_call(
        functools.partial(_inproj_body, tiles_per_seq=tiles_per_seq, prompt_tiles=prompt_tiles),
        grid=(prompt_tiles + ts // TM,),
        in_specs=[
            _prompt_rows(prompt_tiles, D_MODEL),
            _sample_rows(prompt_tiles, D_MODEL),
            _resident((1, D_MODEL)),
            _layer_resident(layer, (D_MODEL, wide)),
            _resident((len(POOL_WINDOWS), POOL_GROUP_WIDTH, POOL_GROUP_WIDTH)),
            _resident((1, POOL_WIDTH)),
            pl.BlockSpec(memory_space=pl.ANY),
            pl.BlockSpec(memory_space=pl.ANY),
        ],
        out_specs=[
            _prompt_rows(prompt_tiles, ATTN_WIDTH),
            _prompt_rows(prompt_tiles, ATTN_WIDTH),
            _prompt_rows(prompt_tiles, ATTN_WIDTH),
            _prompt_rows(prompt_tiles, POOL_WIDTH),
            pl.BlockSpec((None, None, ATTN_WIDTH, TM), trow),
            pl.BlockSpec((None, None, ATTN_WIDTH, TM), trow),
            pl.BlockSpec((None, HALO, POOL_WIDTH),
                         lambda i: (jnp.minimum(i, prompt_tiles - 1) // tiles_per_seq, 0, 0)),
            _sample_rows(prompt_tiles, ATTN_WIDTH),
            _sample_rows(prompt_tiles, ATTN_WIDTH),
            _sample_rows(prompt_tiles, ATTN_WIDTH),
            _sample_rows(prompt_tiles, POOL_WIDTH),
        ],
        out_shape=[
            jax.ShapeDtypeStruct((tp, ATTN_WIDTH), BF16),
            jax.ShapeDtypeStruct((tp, ATTN_WIDTH), BF16),
            jax.ShapeDtypeStruct((tp, ATTN_WIDTH), BF16),
            jax.ShapeDtypeStruct((tp, POOL_WIDTH), BF16),
            stack_shape,
            stack_shape,
            jax.ShapeDtypeStruct((batch, HALO, POOL_WIDTH), F32),
            jax.ShapeDtypeStruct((ts, ATTN_WIDTH), BF16),
            jax.ShapeDtypeStruct((ts, ATTN_WIDTH), F32),
            jax.ShapeDtypeStruct((ts, ATTN_WIDTH), F32),
            jax.ShapeDtypeStruct((ts, POOL_WIDTH), F32),
        ],
        input_output_aliases={6: 4, 7: 5},
        scratch_shapes=[pltpu.VMEM((HALO, POOL_WIDTH), F32)],
        compiler_params=pltpu.CompilerParams(
            dimension_semantics=("arbitrary",), vmem_limit_bytes=48 * 1024 * 1024),
        name="inproj",
    )(xp, xs, g, w_in, pool_w, pool_scale, *stacks)


def _attn_body(bias_ref, q_ref, k_ref, v_ref, o_ref):
    g = pl.program_id(1)
    i = pl.program_id(2)
    first_q = lax.broadcasted_iota(jnp.int32, (TQ, LANES), 1) < HEAD_DIM
    first_k = lax.broadcasted_iota(jnp.int32, (TK, LANES), 1) < HEAD_DIM
    causal = (lax.broadcasted_iota(jnp.int32, (TQ, TK), 1)
              < lax.broadcasted_iota(jnp.int32, (TQ, TK), 0))
    later = _later_keys(TK)
    qs, biases = [], []
    for p in range(HEAD_PAIRS):
        q = q_ref[:, p * LANES:(p + 1) * LANES]
        zeros_q = jnp.zeros_like(q)
        qs.append((jnp.where(first_q, q, zeros_q), jnp.where(first_q, zeros_q, q)))
        head = 2 * (g * HEAD_PAIRS + p)
        biases.append((bias_ref[head], bias_ref[head + 1]))

    def tile(j, masked, state):
        rows = pl.ds(pl.multiple_of(j * TK, TK), TK)
        new_state = []
        for p in range(HEAD_PAIRS):
            acc, carries = state[p]
            k = k_ref[rows, p * LANES:(p + 1) * LANES]
            v = v_ref[rows, p * LANES:(p + 1) * LANES]
            zeros_v = jnp.zeros_like(v)
            ws, new_carries = [], []
            for hh in range(2):
                z = lax.dot_general(qs[p][hh], k, (((1,), (1,)), ((), ())),
                                    preferred_element_type=F32) + biases[p][hh]
                log_beta, log_1mb = _log_sigmoids(z)
                if masked:
                    log_1mb = jnp.where(causal, log_1mb, 0.0)
                stay = _sum_over_later_keys(log_1mb, later) + carries[hh]
                w = jnp.exp(log_beta + stay)
                if masked:
                    w = jnp.where(causal, w, 0.0)
                ws.append(w.astype(BF16))
                new_carries.append(carries[hh] + jnp.sum(log_1mb, axis=1, keepdims=True))
            v2 = jnp.concatenate([jnp.where(first_k, v, zeros_v), jnp.where(first_k, zeros_v, v)], axis=0)
            acc = acc + jnp.dot(jnp.concatenate(ws, axis=1), v2, preferred_element_type=F32)
            new_state.append((acc, tuple(new_carries)))
        return tuple(new_state)

    zero_carry = jnp.zeros((TQ, 1), F32)
    init = tuple((jnp.zeros((TQ, LANES), F32), (zero_carry, zero_carry)) for _ in range(HEAD_PAIRS))
    state = tile(i, True, init)
    state = lax.fori_loop(0, i, lambda jj, st: tile(i - 1 - jj, False, st), state)
    for p in range(HEAD_PAIRS):
        o_ref[:, p * LANES:(p + 1) * LANES] = state[p][0].astype(BF16)


def _prompt_attn(q, kb, vb, bias, batch, seq_len):
    nq = seq_len // TQ
    width = HEAD_PAIRS * LANES
    return pl.pallas_call(
        _attn_body,
        grid=(batch, ATTN_WIDTH // width, nq),
        in_specs=[
            pl.BlockSpec(memory_space=pltpu.SMEM),
            pl.BlockSpec((TQ, width), lambda b, g, i: (b * nq + i, g)),
            pl.BlockSpec((seq_len, width), lambda b, g, i: (b, g)),
            pl.BlockSpec((seq_len, width), lambda b, g, i: (b, g)),
        ],
        out_specs=pl.BlockSpec((TQ, width), lambda b, g, i: (b * nq + i, g)),
        out_shape=jax.ShapeDtypeStruct((batch * seq_len, ATTN_WIDTH), BF16),
        compiler_params=pltpu.CompilerParams(
            dimension_semantics=("arbitrary", "arbitrary", "arbitrary"),
            vmem_limit_bytes=48 * 1024 * 1024),
        name="prompt_attn",
    )(bias, q, kb, vb)


def _decode_body(pt_ref, q_ref, kn_ref, vn_ref, un_ref, st_ref, bias_ref, pw_ref, ps_ref,
                 ck_hbm, cv_hbm, o_ref, kbuf, vbuf, sem, *, layer, n_pages, page_size):
    step = pl.program_id(0)
    n_steps = pl.num_programs(0)
    slot = lax.rem(step, 2)

    def page_copies(of_step, sl):
        cps = []
        for s in range(SEQS_PER_STEP):
            for p in range(n_pages):
                page = pt_ref[of_step * SEQS_PER_STEP + s, p]
                cps.append(pltpu.make_async_copy(ck_hbm.at[layer, page], kbuf.at[sl, s, p], sem.at[0, sl]))
                cps.append(pltpu.make_async_copy(cv_hbm.at[layer, page], vbuf.at[sl, s, p], sem.at[1, sl]))
        return cps

    @pl.when(step == 0)
    def _():
        for cp in page_copies(0, 0):
            cp.start()

    @pl.when(step + 1 < n_steps)
    def _():
        for cp in page_copies(step + 1, 1 - slot):
            cp.start()

    new_keys = [_attend_new_keys(q_ref[s], kn_ref[s], vn_ref[s], bias_ref[...]) for s in range(SEQS_PER_STEP)]
    for cp in page_copies(step, slot):
        cp.wait()
    attended = _attend_past_keys(new_keys, bias_ref[...], [kbuf.at[slot, s] for s in range(SEQS_PER_STEP)],
                                 [vbuf.at[slot, s] for s in range(SEQS_PER_STEP)], n_pages, page_size)
    for s in range(SEQS_PER_STEP):
        o_ref[s, :, :ATTN_WIDTH] = attended[s]
        o_ref[s, :, ATTN_WIDTH:] = _sample_pool(un_ref[s], st_ref[s], pw_ref, ps_ref[...])


def _attend_new_keys(q, kn, vn, bias):
    dec_seq = q.shape[0]
    n_rows = N_HEADS * dec_seq

    q = q.astype(F32)
    sel = (lax.broadcasted_iota(jnp.int32, (n_rows, dec_seq), 0) % dec_seq
           == lax.broadcasted_iota(jnp.int32, (n_rows, dec_seq), 1)).astype(F32)
    q_rows = jnp.dot(sel, q, preferred_element_type=F32)
    row_head = lax.broadcasted_iota(jnp.int32, (n_rows, ATTN_WIDTH), 0) // dec_seq
    col_head = lax.broadcasted_iota(jnp.int32, (n_rows, ATTN_WIDTH), 1) // HEAD_DIM
    qbd = jnp.where(row_head == col_head, q_rows, 0.0).astype(BF16)

    pad = jnp.zeros((LANES - dec_seq, ATTN_WIDTH), F32)
    kn = jnp.concatenate([kn, pad], axis=0).astype(BF16)
    vn = jnp.concatenate([vn, pad], axis=0).astype(BF16)
    zn = lax.dot_general(qbd, kn, (((1,), (1,)), ((), ())),
                         preferred_element_type=F32) + bias[:, :LANES]
    mask_n = (lax.broadcasted_iota(jnp.int32, (n_rows, LANES), 1)
              < lax.broadcasted_iota(jnp.int32, (n_rows, LANES), 0) % dec_seq)
    log_beta_n, log_1mb_n = _log_sigmoids(zn)
    log_1mb_n = jnp.where(mask_n, log_1mb_n, 0.0)
    stay_n = _sum_over_later_keys(log_1mb_n, _later_keys(LANES))
    w_n = jnp.where(mask_n, jnp.exp(log_beta_n + stay_n), 0.0).astype(BF16)
    o_new = jnp.dot(w_n, vn, preferred_element_type=F32)
    return qbd, o_new, jnp.sum(log_1mb_n, axis=1, keepdims=True)


def _attend_past_keys(new_keys, bias, kbufs, vbufs, n_pages, page_size):
    n_seqs = len(new_keys)
    qbds = [nk[0] for nk in new_keys]
    carries = [nk[2] for nk in new_keys]
    n_rows = qbds[0].shape[0]
    dec_seq = n_rows // N_HEADS
    pages_per_block = KB // page_size

    later = _later_keys(KB)
    n_blocks = n_pages // pages_per_block
    ws = [[None] * n_blocks for _ in range(n_seqs)]
    for jb in reversed(range(n_blocks)):
        pages = range(jb * pages_per_block, (jb + 1) * pages_per_block)
        for s in range(n_seqs):
            kblk = jnp.concatenate([kbufs[s][p] for p in pages], axis=1).astype(BF16)
            z = jnp.dot(qbds[s], kblk, preferred_element_type=F32) + bias
            log_beta, log_1mb = _log_sigmoids(z)
            stay = _sum_over_later_keys(log_1mb, later) + carries[s]
            ws[s][jb] = jnp.exp(log_beta + stay).astype(BF16)
            carries[s] = carries[s] + jnp.sum(log_1mb, axis=1, keepdims=True)

    lane = lax.broadcasted_iota(jnp.int32, (dec_seq, LANES), 1)
    outs = []
    for s in range(n_seqs):
        w = jnp.concatenate(ws[s], axis=1)
        w = jnp.concatenate([w, jnp.zeros((LANES - n_rows, w.shape[1]), BF16)], axis=0)
        vt = jnp.concatenate([vbufs[s][p] for p in range(n_pages)], axis=1).astype(BF16)
        acc = lax.dot_general(w, vt, (((1,), (1,)), ((), ())),
                              preferred_element_type=F32)[:n_rows] + new_keys[s][1]
        out = []
        for cb in range(ATTN_WIDTH // LANES):
            cols = slice(cb * LANES, (cb + 1) * LANES)
            h0 = 2 * cb
            a0 = acc[h0 * dec_seq:(h0 + 1) * dec_seq, cols]
            a1 = acc[(h0 + 1) * dec_seq:(h0 + 2) * dec_seq, cols]
            out.append(jnp.where(lane < HEAD_DIM, a0, a1))
        outs.append(jnp.concatenate(out, axis=1))
    return outs


def _sample_pool(un, st, pw_ref, ps):
    ext = jnp.concatenate([st, un], axis=0)
    out = []
    for g, win in enumerate(POOL_WINDOWS):
        cols = slice(g * POOL_GROUP_WIDTH, (g + 1) * POOL_GROUP_WIDTH)
        s = ext[:, cols]
        step = 1
        while step < win:
            s = s + pltpu.roll(s, step, axis=0)
            step *= 2
        d = (s[HALO:] / float(win) - un[:, cols]).astype(BF16).astype(F32)
        out.append(jnp.dot(d, pw_ref[g], preferred_element_type=F32) * ps[:, cols])
    return jnp.concatenate(out, axis=1)


def _decode(page_table, q, kn, vn, un, st, bias_rows, pool_w, pool_scale, cache_kt, cache_vt, layer):
    nb, dec_seq, _ = q.shape
    assert nb % SEQS_PER_STEP == 0
    n_pages = page_table.shape[1]
    page_size = cache_kt.shape[-1]
    n_rows = N_HEADS * dec_seq
    assert KB % page_size == 0 and n_pages % (KB // page_size) == 0 and page_size == LANES
    seq = lambda b, pt: (b, 0, 0)
    const2 = lambda b, pt: (0, 0)
    const3 = lambda b, pt: (0, 0, 0)
    grid_spec = pltpu.PrefetchScalarGridSpec(
        num_scalar_prefetch=1,
        grid=(nb // SEQS_PER_STEP,),
        in_specs=[
            pl.BlockSpec((SEQS_PER_STEP, dec_seq, ATTN_WIDTH), seq),
            pl.BlockSpec((SEQS_PER_STEP, dec_seq, ATTN_WIDTH), seq),
            pl.BlockSpec((SEQS_PER_STEP, dec_seq, ATTN_WIDTH), seq),
            pl.BlockSpec((SEQS_PER_STEP, dec_seq, POOL_WIDTH), seq),
            pl.BlockSpec((SEQS_PER_STEP, HALO, POOL_WIDTH), seq),
            pl.BlockSpec((n_rows, KB), const2),
            pl.BlockSpec((len(POOL_WINDOWS), POOL_GROUP_WIDTH, POOL_GROUP_WIDTH), const3),
            pl.BlockSpec((1, POOL_WIDTH), const2),
            pl.BlockSpec(memory_space=pl.ANY),
            pl.BlockSpec(memory_space=pl.ANY),
        ],
        out_specs=pl.BlockSpec((SEQS_PER_STEP, dec_seq, ATTN_WIDTH + POOL_WIDTH), seq),
        scratch_shapes=[
            pltpu.VMEM((2, SEQS_PER_STEP, n_pages, ATTN_WIDTH, page_size), F32),
            pltpu.VMEM((2, SEQS_PER_STEP, n_pages, ATTN_WIDTH, page_size), F32),
            pltpu.SemaphoreType.DMA((2, 2)),
        ],
    )
    return pl.pallas_call(
        functools.partial(_decode_body, layer=layer, n_pages=n_pages, page_size=page_size),
        grid_spec=grid_spec,
        out_shape=jax.ShapeDtypeStruct((nb, dec_seq, ATTN_WIDTH + POOL_WIDTH), F32),
        compiler_params=pltpu.CompilerParams(
            dimension_semantics=("arbitrary",), vmem_limit_bytes=56 * 1024 * 1024),
        name="sample_attn",
    )(page_table, q, kn, vn, un, st, bias_rows, pool_w, pool_scale, cache_kt, cache_vt)


def _outproj_body(xp_ref, xs_ref, oa_ref, ob_ref, dec_ref, w_ref, g_ref, yp_ref, ys_ref, *, prompt_tiles):
    def tile(mix_refs, x_ref, y_ref):
        mix = jnp.concatenate([r[...].astype(BF16) for r in mix_refs], axis=1)
        y = jnp.dot(mix, w_ref[...], preferred_element_type=F32)
        y_ref[...] = x_ref[...] + _rms(y, g_ref[...])

    _per_row_kind(prompt_tiles, tile, ((oa_ref, ob_ref), xp_ref, yp_ref), ((dec_ref,), xs_ref, ys_ref))


def _outproj(xp, xs, oa, ob, dec, w_out, post_g, layer):
    prompt_tiles = xp.shape[0] // TM
    return pl.pallas_call(
        functools.partial(_outproj_body, prompt_tiles=prompt_tiles),
        grid=(prompt_tiles + xs.shape[0] // TM,),
        in_specs=[
            _prompt_rows(prompt_tiles, D_MODEL),
            _sample_rows(prompt_tiles, D_MODEL),
            _prompt_rows(prompt_tiles, ATTN_WIDTH),
            _prompt_rows(prompt_tiles, POOL_WIDTH),
            _sample_rows(prompt_tiles, D_MODEL),
            _layer_resident(layer, (D_MODEL, D_MODEL)),
            _resident((1, D_MODEL)),
        ],
        out_specs=[_prompt_rows(prompt_tiles, D_MODEL), _sample_rows(prompt_tiles, D_MODEL)],
        out_shape=_row_pair_shapes(xp, xs),
        compiler_params=pltpu.CompilerParams(
            dimension_semantics=("arbitrary",), vmem_limit_bytes=48 * 1024 * 1024),
        name="outproj",
    )(xp, xs, oa, ob, dec, w_out, post_g)


def kernel(x_prompt, x_sample, cache_k, cache_v, state_pool, page_table, w_in, w_out, sb_bias, pool_w, pool_scale, ffn1_pre_g, ffn1_post_g, ffn1_w_gate, ffn1_w_up, ffn1_w_down, mix_pre_g, mix_post_g, ffn2_pre_g, ffn2_post_g, ffn2_w_gate, ffn2_w_up, ffn2_w_down):
    batch, seq_len, _ = x_prompt.shape
    dec_batch, dec_seq, _ = x_sample.shape
    depth = w_in.shape[0]
    tp = batch * seq_len
    ts = dec_batch * dec_seq
    assert seq_len % TM == 0 and seq_len % TQ == 0 and ts % TM == 0
    assert N_HEADS * dec_seq <= LANES

    n_pool_pages, page_size = cache_k.shape[1], cache_k.shape[2]
    ckt = cache_k.transpose(0, 1, 3, 4, 2).reshape(depth, n_pool_pages, ATTN_WIDTH, page_size)
    cvt = cache_v.transpose(0, 1, 3, 4, 2).reshape(depth, n_pool_pages, ATTN_WIDTH, page_size)
    st_pad = jnp.pad(state_pool, ((0, 0), (0, 0), (HALO - POOL_BUF, 0), (0, 0)))

    xp, xs = x_prompt.reshape(tp, D_MODEL), x_sample.reshape(ts, D_MODEL)
    row = lambda a, l: a[l][None, :]
    per_seq = lambda a: a.reshape(dec_batch, dec_seq, a.shape[-1])
    heads = lambda a: a.reshape(dec_batch, dec_seq, N_HEADS, HEAD_DIM)
    stacks = tuple(jnp.zeros((depth, batch, ATTN_WIDTH, seq_len), F32) for _ in range(2))
    pool_p, k_s, v_s, pool_s = [], [], [], []
    ffn1_w = [w.astype(BF16) for w in (ffn1_w_gate, ffn1_w_up, ffn1_w_down)]
    ffn2_w = [w.astype(BF16) for w in (ffn2_w_gate, ffn2_w_up, ffn2_w_down)]
    w_in_b, w_out_b = w_in.astype(BF16), w_out.astype(BF16)
    for l in range(depth):
        xp, xs = _ffn(xp, xs, row(ffn1_pre_g, l), *ffn1_w, row(ffn1_post_g, l), l)

        pw = pool_w[l].astype(BF16)
        ps = row(pool_scale, l)
        q, kb, vb, ob, kt_stack, vt_stack, tail, qs, ks, vs, us = _inproj(
            xp, xs, row(mix_pre_g, l), w_in_b, pw, ps, stacks, l, batch, seq_len)
        stacks = (kt_stack, vt_stack)
        oa = _prompt_attn(q, kb, vb, sb_bias[l], batch, seq_len)

        bias_rows = jnp.broadcast_to(jnp.repeat(sb_bias[l], dec_seq)[:, None], (N_HEADS * dec_seq, KB))
        dec = _decode(page_table, per_seq(qs), per_seq(ks), per_seq(vs), per_seq(us), st_pad[l],
                      bias_rows, pw.astype(F32), ps, ckt, cvt, l)

        xp, xs = _outproj(xp, xs, oa, ob, dec.reshape(ts, D_MODEL), w_out_b, row(mix_post_g, l), l)

        xp, xs = _ffn(xp, xs, row(ffn2_pre_g, l), *ffn2_w, row(ffn2_post_g, l), l)

        pool_p.append(tail[:, HALO - POOL_BUF:])
        k_s.append(heads(ks))
        v_s.append(heads(vs))
        pool_s.append(jnp.concatenate([state_pool[l], per_seq(us)], axis=1)[:, dec_seq:])

    y_prompt = xp.reshape(batch, seq_len, D_MODEL)
    y_sample = xs.reshape(dec_batch, dec_seq, D_MODEL)
    seq_major = lambda a: a.reshape(depth, batch, N_HEADS, HEAD_DIM, seq_len).transpose(0, 1, 4, 2, 3)
    return (y_prompt, y_sample, seq_major(stacks[0]), seq_major(stacks[1]), jnp.stack(pool_p),
            jnp.stack(k_s), jnp.stack(v_s), jnp.stack(pool_s))
```

```python
import functools

import jax
import jax.numpy as jnp
from jax import lax
from jax.experimental import pallas as pl
from jax.experimental.pallas import tpu as pltpu

D_MODEL = 1024
D_FF = 2816
ATTN_WIDTH = 512
POOL_WIDTH = 512
HEAD_DIM = 64
N_HEADS = 8
POOL_WINDOWS = (2, 4, 8, 16)
POOL_GROUP_WIDTH = 128
POOL_BUF = 15
RMS_EPS = 1e-6
FFN_RES_WEIGHT = 0.5

LANES = 128
HALO = 16
TM = 512
FF_CHUNK = 256
TQ = 256
TK = 256
KB = 256
SEQS_PER_STEP = 2
HEAD_PAIRS = 4

F32 = jnp.float32
BF16 = jnp.bfloat16


def _rms(x, g):
    ms = jnp.mean(x * x, axis=-1, keepdims=True)
    return (x * lax.rsqrt(ms + RMS_EPS)) * g


def _log_sigmoids(z):
    log_beta = jnp.minimum(z, 0.0) - jnp.log(1.0 + jnp.exp(-jnp.abs(z)))
    return log_beta, log_beta - z


def _later_keys(n):
    later = (lax.broadcasted_iota(jnp.int32, (n, n), 0)
             > lax.broadcasted_iota(jnp.int32, (n, n), 1)).astype(BF16)
    return jnp.concatenate([later, later], axis=0)


def _sum_over_later_keys(log_1mb, later2):
    hi = log_1mb.astype(BF16)
    lo = (log_1mb - hi.astype(F32)).astype(BF16)
    return jnp.dot(jnp.concatenate([hi, lo], axis=1), later2, preferred_element_type=F32)


def _prompt_rows(prompt_tiles, width):
    return pl.BlockSpec((TM, width), lambda i: (jnp.minimum(i, prompt_tiles - 1), 0))


def _sample_rows(prompt_tiles, width):
    return pl.BlockSpec((TM, width), lambda i: (jnp.maximum(i - prompt_tiles, 0), 0))


def _per_row_kind(prompt_tiles, tile_fn, prompt_refs, sample_refs):
    @pl.when(pl.program_id(0) < prompt_tiles)
    def _():
        tile_fn(*prompt_refs)

    @pl.when(pl.program_id(0) >= prompt_tiles)
    def _():
        tile_fn(*sample_refs)


def _row_pair_shapes(xp, xs):
    return [jax.ShapeDtypeStruct(xp.shape, xp.dtype), jax.ShapeDtypeStruct(xs.shape, xs.dtype)]


def _ffn_body(xp_ref, xs_ref, pre_ref, wg_ref, wu_ref, wd_ref, post_ref, yp_ref, ys_ref, a_ref,
              *, prompt_tiles):
    def tile(x_ref, y_ref):
        x = x_ref[...]
        h = _rms(x, pre_ref[...]).astype(BF16)
        for c in range(D_FF // FF_CHUNK):
            cols = pl.ds(c * FF_CHUNK, FF_CHUNK)
            g = jnp.dot(h, wg_ref[:, cols], preferred_element_type=F32)
            u = jnp.dot(h, wu_ref[:, cols], preferred_element_type=F32)
            a_ref[:, cols] = (g / (1.0 + jnp.exp(-g)) * u).astype(BF16)
        y = jnp.dot(a_ref[...], wd_ref[...], preferred_element_type=F32)
        y_ref[...] = x + FFN_RES_WEIGHT * _rms(y, post_ref[...])

    _per_row_kind(prompt_tiles, tile, (xp_ref, yp_ref), (xs_ref, ys_ref))


def _resident(shape):
    return pl.BlockSpec(shape, lambda i: (0,) * len(shape), pipeline_mode=pl.Buffered(1))


def _layer_resident(layer, shape):
    return pl.BlockSpec((None,) + shape, lambda i: (layer,) + (0,) * len(shape),
                        pipeline_mode=pl.Buffered(1))


def _ffn(xp, xs, pre_g, wg, wu, wd, post_g, layer):
    prompt_tiles = xp.shape[0] // TM
    return pl.pallas_call(
        functools.partial(_ffn_body, prompt_tiles=prompt_tiles),
        grid=(prompt_tiles + xs.shape[0] // TM,),
        in_specs=[
            _prompt_rows(prompt_tiles, D_MODEL),
            _sample_rows(prompt_tiles, D_MODEL),
            _resident((1, D_MODEL)),
            _layer_resident(layer, (D_MODEL, D_FF)),
            _layer_resident(layer, (D_MODEL, D_FF)),
            _layer_resident(layer, (D_FF, D_MODEL)),
            _resident((1, D_MODEL)),
        ],
        out_specs=[_prompt_rows(prompt_tiles, D_MODEL), _sample_rows(prompt_tiles, D_MODEL)],
        out_shape=_row_pair_shapes(xp, xs),
        scratch_shapes=[pltpu.VMEM((TM, D_FF), BF16)],
        compiler_params=pltpu.CompilerParams(
            dimension_semantics=("arbitrary",), vmem_limit_bytes=48 * 1024 * 1024),
        name="ffn",
    )(xp, xs, pre_g, wg, wu, wd, post_g)


def _inproj_body(xp_ref, xs_ref, g_ref, w_ref, pw_ref, ps_ref, kt_in_ref, vt_in_ref,
                 q_ref, kb_ref, vb_ref, ob_ref, kt_ref, vt_ref, tail_ref,
                 qs_ref, ks_ref, vs_ref, us_ref, halo_ref, *, tiles_per_seq, prompt_tiles):
    del kt_in_ref, vt_in_ref
    i = pl.program_id(0)

    def project(x_ref):
        h = _rms(x_ref[...], g_ref[...]).astype(BF16)
        proj = jnp.dot(h, w_ref[...], preferred_element_type=F32)
        q = (proj[:, 0:ATTN_WIDTH] * (HEAD_DIM ** -0.5)).astype(BF16)
        return (q, proj[:, ATTN_WIDTH:2 * ATTN_WIDTH], proj[:, 2 * ATTN_WIDTH:3 * ATTN_WIDTH],
                proj[:, 3 * ATTN_WIDTH:])

    @pl.when(i >= prompt_tiles)
    def _():
        qs_ref[...], ks_ref[...], vs_ref[...], us_ref[...] = project(xs_ref)

    @pl.when(i < prompt_tiles)
    def _():
        q, k, v, u = project(xp_ref)
        q_ref[...] = q
        kb_ref[...] = k.astype(BF16)
        vb_ref[...] = v.astype(BF16)
        kt_ref[...] = k.T
        vt_ref[...] = v.T

        tile_in_seq = lax.rem(i, tiles_per_seq)

        @pl.when(tile_in_seq == 0)
        def _():
            halo_ref[...] = jnp.zeros_like(halo_ref)

        pos = tile_in_seq * TM + lax.broadcasted_iota(jnp.int32, (TM, LANES), 0)
        for g, win in enumerate(POOL_WINDOWS):
            cols = slice(g * POOL_GROUP_WIDTH, (g + 1) * POOL_GROUP_WIDTH)
            ug = u[:, cols]
            s = jnp.concatenate([halo_ref[:, cols], ug], axis=0)
            step = 1
            while step < win:
                s = s + pltpu.roll(s, step, axis=0)
                step *= 2
            cnt = jnp.minimum(pos + 1, win).astype(F32)
            d = (s[HALO:] / cnt - ug).astype(BF16)
            o = jnp.dot(d, pw_ref[g], preferred_element_type=F32)
            ob_ref[:, cols] = (o * ps_ref[:, cols]).astype(BF16)
        halo_ref[...] = u[TM - HALO:, :]

        @pl.when(tile_in_seq == tiles_per_seq - 1)
        def _():
            tail_ref[...] = u[TM - HALO:, :]


def _inproj(xp, xs, g, w_in, pool_w, pool_scale, stacks, layer, batch, seq_len):
    tp, ts = xp.shape[0], xs.shape[0]
    tiles_per_seq = seq_len // TM
    prompt_tiles = tp // TM
    wide = 3 * ATTN_WIDTH + POOL_WIDTH

    def trow(i):
        ip = jnp.minimum(i, prompt_tiles - 1)
        return (layer, ip // tiles_per_seq, 0, ip % tiles_per_seq)

    stack_shape = jax.ShapeDtypeStruct(stacks[0].shape, F32)
    return pl.pallas_call(
        functools.partial(_inproj_body, tiles_per_seq=tiles_per_seq, prompt_tiles=prompt_tiles),
        grid=(prompt_tiles + ts // TM,),
        in_specs=[
            _prompt_rows(prompt_tiles, D_MODEL),
            _sample_rows(prompt_tiles, D_MODEL),
            _resident((1, D_MODEL)),
            _layer_resident(layer, (D_MODEL, wide)),
            _resident((len(POOL_WINDOWS), POOL_GROUP_WIDTH, POOL_GROUP_WIDTH)),
            _resident((1, POOL_WIDTH)),
            pl.BlockSpec(memory_space=pl.ANY),
            pl.BlockSpec(memory_space=pl.ANY),
        ],
        out_specs=[
            _prompt_rows(prompt_tiles, ATTN_WIDTH),
            _prompt_rows(prompt_tiles, ATTN_WIDTH),
            _prompt_rows(prompt_tiles, ATTN_WIDTH),
            _prompt_rows(prompt_tiles, POOL_WIDTH),
            pl.BlockSpec((None, None, ATTN_WIDTH, TM), trow),
            pl.BlockSpec((None, None, ATTN_WIDTH, TM), trow),
            pl.BlockSpec((None, HALO, POOL_WIDTH),
                         lambda i: (jnp.minimum(i, prompt_tiles - 1) // tiles_per_seq, 0, 0)),
            _sample_rows(prompt_tiles, ATTN_WIDTH),
            _sample_rows(prompt_tiles, ATTN_WIDTH),
            _sample_rows(prompt_tiles, ATTN_WIDTH),
            _sample_rows(prompt_tiles, POOL_WIDTH),
        ],
        out_shape=[
            jax.ShapeDtypeStruct((tp, ATTN_WIDTH), BF16),
            jax.ShapeDtypeStruct((tp, ATTN_WIDTH), BF16),
            jax.ShapeDtypeStruct((tp, ATTN_WIDTH), BF16),
            jax.ShapeDtypeStruct((tp, POOL_WIDTH), BF16),
            stack_shape,
            stack_shape,
            jax.ShapeDtypeStruct((batch, HALO, POOL_WIDTH), F32),
            jax.ShapeDtypeStruct((ts, ATTN_WIDTH), BF16),
            jax.ShapeDtypeStruct((ts, ATTN_WIDTH), F32),
            jax.ShapeDtypeStruct((ts, ATTN_WIDTH), F32),
            jax.ShapeDtypeStruct((ts, POOL_WIDTH), F32),
        ],
        input_output_aliases={6: 4, 7: 5},
        scratch_shapes=[pltpu.VMEM((HALO, POOL_WIDTH), F32)],
        compiler_params=pltpu.CompilerParams(
            dimension_semantics=("arbitrary",), vmem_limit_bytes=48 * 1024 * 1024),
        name="inproj",
    )(xp, xs, g, w_in, pool_w, pool_scale, *stacks)


def _attn_body(bias_ref, q_ref, k_ref, v_ref, o_ref):
    g = pl.program_id(1)
    i = pl.program_id(2)
    first_q = lax.broadcasted_iota(jnp.int32, (TQ, LANES), 1) < HEAD_DIM
    first_k = lax.broadcasted_iota(jnp.int32, (TK, LANES), 1) < HEAD_DIM
    causal = (lax.broadcasted_iota(jnp.int32, (TQ, TK), 1)
              < lax.broadcasted_iota(jnp.int32, (TQ, TK), 0))
    later = _later_keys(TK)
    qs, biases = [], []
    for p in range(HEAD_PAIRS):
        q = q_ref[:, p * LANES:(p + 1) * LANES]
        zeros_q = jnp.zeros_like(q)
        qs.append((jnp.where(first_q, q, zeros_q), jnp.where(first_q, zeros_q, q)))
        head = 2 * (g * HEAD_PAIRS + p)
        biases.append((bias_ref[head], bias_ref[head + 1]))

    def tile(j, masked, state):
        rows = pl.ds(pl.multiple_of(j * TK, TK), TK)
        new_state = []
        for p in range(HEAD_PAIRS):
            acc, carries = state[p]
            k = k_ref[rows, p * LANES:(p + 1) * LANES]
            v = v_ref[rows, p * LANES:(p + 1) * LANES]
            zeros_v = jnp.zeros_like(v)
            ws, new_carries = [], []
            for hh in range(2):
                z = lax.dot_general(qs[p][hh], k, (((1,), (1,)), ((), ())),
                                    preferred_element_type=F32) + biases[p][hh]
                log_beta, log_1mb = _log_sigmoids(z)
                if masked:
                    log_1mb = jnp.where(causal, log_1mb, 0.0)
                later_sum = _sum_over_later_keys(log_1mb, later)
                w = jnp.exp(log_beta + (later_sum + carries[hh]))
                if masked:
                    w = jnp.where(causal, w, 0.0)
                ws.append(w.astype(BF16))
                new_carries.append(carries[hh] + (later_sum[:, :1] + log_1mb[:, :1]))
            v2 = jnp.concatenate([jnp.where(first_k, v, zeros_v), jnp.where(first_k, zeros_v, v)], axis=0)
            acc = acc + jnp.dot(jnp.concatenate(ws, axis=1), v2, preferred_element_type=F32)
            new_state.append((acc, tuple(new_carries)))
        return tuple(new_state)

    zero_carry = jnp.zeros((TQ, 1), F32)
    init = tuple((jnp.zeros((TQ, LANES), F32), (zero_carry, zero_carry)) for _ in range(HEAD_PAIRS))
    state = tile(i, True, init)
    state = lax.fori_loop(0, i, lambda jj, st: tile(i - 1 - jj, False, st), state)
    for p in range(HEAD_PAIRS):
        o_ref[:, p * LANES:(p + 1) * LANES] = state[p][0].astype(BF16)


def _prompt_attn(q, kb, vb, bias, batch, seq_len):
    nq = seq_len // TQ
    width = HEAD_PAIRS * LANES
    return pl.pallas_call(
        _attn_body,
        grid=(batch, ATTN_WIDTH // width, nq),
        in_specs=[
            pl.BlockSpec(memory_space=pltpu.SMEM),
            pl.BlockSpec((TQ, width), lambda b, g, i: (b * nq + i, g)),
            pl.BlockSpec((seq_len, width), lambda b, g, i: (b, g)),
            pl.BlockSpec((seq_len, width), lambda b, g, i: (b, g)),
        ],
        out_specs=pl.BlockSpec((TQ, width), lambda b, g, i: (b * nq + i, g)),
        out_shape=jax.ShapeDtypeStruct((batch * seq_len, ATTN_WIDTH), BF16),
        compiler_params=pltpu.CompilerParams(
            dimension_semantics=("arbitrary", "arbitrary", "arbitrary"),
            vmem_limit_bytes=48 * 1024 * 1024),
        name="prompt_attn",
    )(bias, q, kb, vb)


def _decode_body(pt_ref, q_ref, kn_ref, vn_ref, un_ref, st_ref, bias_ref, pw_ref, ps_ref,
                 ck_hbm, cv_hbm, o_ref, kbuf, vbuf, sem, *, layer, n_pages, page_size):
    step = pl.program_id(0)
    n_steps = pl.num_programs(0)
    slot = lax.rem(step, 2)

    def page_copies(of_step, sl):
        cps = []
        for s in range(SEQS_PER_STEP):
            for p in range(n_pages):
                page = pt_ref[of_step * SEQS_PER_STEP + s, p]
                cps.append(pltpu.make_async_copy(ck_hbm.at[layer, page], kbuf.at[sl, s, p], sem.at[0, sl]))
                cps.append(pltpu.make_async_copy(cv_hbm.at[layer, page], vbuf.at[sl, s, p], sem.at[1, sl]))
        return cps

    @pl.when(step == 0)
    def _():
        for cp in page_copies(0, 0):
            cp.start()

    @pl.when(step + 1 < n_steps)
    def _():
        for cp in page_copies(step + 1, 1 - slot):
            cp.start()

    new_keys = [_attend_new_keys(q_ref[s], kn_ref[s], vn_ref[s], bias_ref[...]) for s in range(SEQS_PER_STEP)]
    for cp in page_copies(step, slot):
        cp.wait()
    attended = _attend_past_keys(new_keys, bias_ref[...], [kbuf.at[slot, s] for s in range(SEQS_PER_STEP)],
                                 [vbuf.at[slot, s] for s in range(SEQS_PER_STEP)], n_pages, page_size)
    for s in range(SEQS_PER_STEP):
        o_ref[s, :, :ATTN_WIDTH] = attended[s]
        o_ref[s, :, ATTN_WIDTH:] = _sample_pool(un_ref[s], st_ref[s], pw_ref, ps_ref[...])


def _attend_new_keys(q, kn, vn, bias):
    dec_seq = q.shape[0]
    n_rows = N_HEADS * dec_seq

    q = q.astype(F32)
    sel = (lax.broadcasted_iota(jnp.int32, (n_rows, dec_seq), 0) % dec_seq
           == lax.broadcasted_iota(jnp.int32, (n_rows, dec_seq), 1)).astype(F32)
    q_rows = jnp.dot(sel, q, preferred_element_type=F32)
    row_head = lax.broadcasted_iota(jnp.int32, (n_rows, ATTN_WIDTH), 0) // dec_seq
    col_head = lax.broadcasted_iota(jnp.int32, (n_rows, ATTN_WIDTH), 1) // HEAD_DIM
    qbd = jnp.where(row_head == col_head, q_rows, 0.0).astype(BF16)

    pad = jnp.zeros((LANES - dec_seq, ATTN_WIDTH), F32)
    kn = jnp.concatenate([kn, pad], axis=0).astype(BF16)
    vn = jnp.concatenate([vn, pad], axis=0).astype(BF16)
    zn = lax.dot_general(qbd, kn, (((1,), (1,)), ((), ())),
                         preferred_element_type=F32) + bias[:, :LANES]
    mask_n = (lax.broadcasted_iota(jnp.int32, (n_rows, LANES), 1)
              < lax.broadcasted_iota(jnp.int32, (n_rows, LANES), 0) % dec_seq)
    log_beta_n, log_1mb_n = _log_sigmoids(zn)
    log_1mb_n = jnp.where(mask_n, log_1mb_n, 0.0)
    stay_n = _sum_over_later_keys(log_1mb_n, _later_keys(LANES))
    w_n = jnp.where(mask_n, jnp.exp(log_beta_n + stay_n), 0.0).astype(BF16)
    o_new = jnp.dot(w_n, vn, preferred_element_type=F32)
    return qbd, o_new, jnp.sum(log_1mb_n, axis=1, keepdims=True)


def _attend_past_keys(new_keys, bias, kbufs, vbufs, n_pages, page_size):
    n_seqs = len(new_keys)
    qbds = [nk[0] for nk in new_keys]
    carries = [nk[2] for nk in new_keys]
    n_rows = qbds[0].shape[0]
    dec_seq = n_rows // N_HEADS
    pages_per_block = KB // page_size

    later = _later_keys(KB)
    n_blocks = n_pages // pages_per_block
    ws = [[None] * n_blocks for _ in range(n_seqs)]
    for jb in reversed(range(n_blocks)):
        pages = range(jb * pages_per_block, (jb + 1) * pages_per_block)
        for s in range(n_seqs):
            kblk = jnp.concatenate([kbufs[s][p] for p in pages], axis=1).astype(BF16)
            z = jnp.dot(qbds[s], kblk, preferred_element_type=F32) + bias
            log_beta, log_1mb = _log_sigmoids(z)
            stay = _sum_over_later_keys(log_1mb, later) + carries[s]
            ws[s][jb] = jnp.exp(log_beta + stay).astype(BF16)
            carries[s] = carries[s] + jnp.sum(log_1mb, axis=1, keepdims=True)

    lane = lax.broadcasted_iota(jnp.int32, (dec_seq, LANES), 1)
    outs = []
    for s in range(n_seqs):
        w = jnp.concatenate(ws[s], axis=1)
        w = jnp.concatenate([w, jnp.zeros((LANES - n_rows, w.shape[1]), BF16)], axis=0)
        vt = jnp.concatenate([vbufs[s][p] for p in range(n_pages)], axis=1).astype(BF16)
        acc = lax.dot_general(w, vt, (((1,), (1,)), ((), ())),
                              preferred_element_type=F32)[:n_rows] + new_keys[s][1]
        out = []
        for cb in range(ATTN_WIDTH // LANES):
            cols = slice(cb * LANES, (cb + 1) * LANES)
            h0 = 2 * cb
            a0 = acc[h0 * dec_seq:(h0 + 1) * dec_seq, cols]
            a1 = acc[(h0 + 1) * dec_seq:(h0 + 2) * dec_seq, cols]
            out.append(jnp.where(lane < HEAD_DIM, a0, a1))
        outs.append(jnp.concatenate(out, axis=1))
    return outs


def _sample_pool(un, st, pw_ref, ps):
    ext = jnp.concatenate([st, un], axis=0)
    out = []
    for g, win in enumerate(POOL_WINDOWS):
        cols = slice(g * POOL_GROUP_WIDTH, (g + 1) * POOL_GROUP_WIDTH)
        s = ext[:, cols]
        step = 1
        while step < win:
            s = s + pltpu.roll(s, step, axis=0)
            step *= 2
        d = (s[HALO:] / float(win) - un[:, cols]).astype(BF16).astype(F32)
        out.append(jnp.dot(d, pw_ref[g], preferred_element_type=F32) * ps[:, cols])
    return jnp.concatenate(out, axis=1)


def _decode(page_table, q, kn, vn, un, st, bias_rows, pool_w, pool_scale, cache_kt, cache_vt, layer):
    nb, dec_seq, _ = q.shape
    assert nb % SEQS_PER_STEP == 0
    n_pages = page_table.shape[1]
    page_size = cache_kt.shape[-1]
    n_rows = N_HEADS * dec_seq
    assert KB % page_size == 0 and n_pages % (KB // page_size) == 0 and page_size == LANES
    seq = lambda b, pt: (b, 0, 0)
    const2 = lambda b, pt: (0, 0)
    const3 = lambda b, pt: (0, 0, 0)
    grid_spec = pltpu.PrefetchScalarGridSpec(
        num_scalar_prefetch=1,
        grid=(nb // SEQS_PER_STEP,),
        in_specs=[
            pl.BlockSpec((SEQS_PER_STEP, dec_seq, ATTN_WIDTH), seq),
            pl.BlockSpec((SEQS_PER_STEP, dec_seq, ATTN_WIDTH), seq),
            pl.BlockSpec((SEQS_PER_STEP, dec_seq, ATTN_WIDTH), seq),
            pl.BlockSpec((SEQS_PER_STEP, dec_seq, POOL_WIDTH), seq),
            pl.BlockSpec((SEQS_PER_STEP, HALO, POOL_WIDTH), seq),
            pl.BlockSpec((n_rows, KB), const2),
            pl.BlockSpec((len(POOL_WINDOWS), POOL_GROUP_WIDTH, POOL_GROUP_WIDTH), const3),
            pl.BlockSpec((1, POOL_WIDTH), const2),
            pl.BlockSpec(memory_space=pl.ANY),
            pl.BlockSpec(memory_space=pl.ANY),
        ],
        out_specs=pl.BlockSpec((SEQS_PER_STEP, dec_seq, ATTN_WIDTH + POOL_WIDTH), seq),
        scratch_shapes=[
            pltpu.VMEM((2, SEQS_PER_STEP, n_pages, ATTN_WIDTH, page_size), F32),
            pltpu.VMEM((2, SEQS_PER_STEP, n_pages, ATTN_WIDTH, page_size), F32),
            pltpu.SemaphoreType.DMA((2, 2)),
        ],
    )
    return pl.pallas_call(
        functools.partial(_decode_body, layer=layer, n_pages=n_pages, page_size=page_size),
        grid_spec=grid_spec,
        out_shape=jax.ShapeDtypeStruct((nb, dec_seq, ATTN_WIDTH + POOL_WIDTH), F32),
        compiler_params=pltpu.CompilerParams(
            dimension_semantics=("arbitrary",), vmem_limit_bytes=56 * 1024 * 1024),
        name="sample_attn",
    )(page_table, q, kn, vn, un, st, bias_rows, pool_w, pool_scale, cache_kt, cache_vt)


def _outproj_body(xp_ref, xs_ref, oa_ref, ob_ref, dec_ref, w_ref, g_ref, yp_ref, ys_ref, *, prompt_tiles):
    def tile(mix_refs, x_ref, y_ref):
        mix = jnp.concatenate([r[...].astype(BF16) for r in mix_refs], axis=1)
        y = jnp.dot(mix, w_ref[...], preferred_element_type=F32)
        y_ref[...] = x_ref[...] + _rms(y, g_ref[...])

    _per_row_kind(prompt_tiles, tile, ((oa_ref, ob_ref), xp_ref, yp_ref), ((dec_ref,), xs_ref, ys_ref))


def _outproj(xp, xs, oa, ob, dec, w_out, post_g, layer):
    prompt_tiles = xp.shape[0] // TM
    return pl.pallas_call(
        functools.partial(_outproj_body, prompt_tiles=prompt_tiles),
        grid=(prompt_tiles + xs.shape[0] // TM,),
        in_specs=[
            _prompt_rows(prompt_tiles, D_MODEL),
            _sample_rows(prompt_tiles, D_MODEL),
            _prompt_rows(prompt_tiles, ATTN_WIDTH),
            _prompt_rows(prompt_tiles, POOL_WIDTH),
            _sample_rows(prompt_tiles, D_MODEL),
            _layer_resident(layer, (D_MODEL, D_MODEL)),
            _resident((1, D_MODEL)),
        ],
        out_specs=[_prompt_rows(prompt_tiles, D_MODEL), _sample_rows(prompt_tiles, D_MODEL)],
        out_shape=_row_pair_shapes(xp, xs),
        compiler_params=pltpu.CompilerParams(
            dimension_semantics=("arbitrary",), vmem_limit_bytes=48 * 1024 * 1024),
        name="outproj",
    )(xp, xs, oa, ob, dec, w_out, post_g)


def kernel(x_prompt, x_sample, cache_k, cache_v, state_pool, page_table, w_in, w_out, sb_bias, pool_w, pool_scale, ffn1_pre_g, ffn1_post_g, ffn1_w_gate, ffn1_w_up, ffn1_w_down, mix_pre_g, mix_post_g, ffn2_pre_g, ffn2_post_g, ffn2_w_gate, ffn2_w_up, ffn2_w_down):
    batch, seq_len, _ = x_prompt.shape
    dec_batch, dec_seq, _ = x_sample.shape
    depth = w_in.shape[0]
    tp = batch * seq_len
    ts = dec_batch * dec_seq
    assert seq_len % TM == 0 and seq_len % TQ == 0 and ts % TM == 0
    assert N_HEADS * dec_seq <= LANES

    n_pool_pages, page_size = cache_k.shape[1], cache_k.shape[2]
    ckt = cache_k.transpose(0, 1, 3, 4, 2).reshape(depth, n_pool_pages, ATTN_WIDTH, page_size)
    cvt = cache_v.transpose(0, 1, 3, 4, 2).reshape(depth, n_pool_pages, ATTN_WIDTH, page_size)
    st_pad = jnp.pad(state_pool, ((0, 0), (0, 0), (HALO - POOL_BUF, 0), (0, 0)))

    xp, xs = x_prompt.reshape(tp, D_MODEL), x_sample.reshape(ts, D_MODEL)
    row = lambda a, l: a[l][None, :]
    per_seq = lambda a: a.reshape(dec_batch, dec_seq, a.shape[-1])
    heads = lambda a: a.reshape(dec_batch, dec_seq, N_HEADS, HEAD_DIM)
    stacks = tuple(jnp.zeros((depth, batch, ATTN_WIDTH, seq_len), F32) for _ in range(2))
    pool_p, k_s, v_s, pool_s = [], [], [], []
    ffn1_w = [w.astype(BF16) for w in (ffn1_w_gate, ffn1_w_up, ffn1_w_down)]
    ffn2_w = [w.astype(BF16) for w in (ffn2_w_gate, ffn2_w_up, ffn2_w_down)]
    w_in_b, w_out_b = w_in.astype(BF16), w_out.astype(BF16)
    for l in range(depth):
        xp, xs = _ffn(xp, xs, row(ffn1_pre_g, l), *ffn1_w, row(ffn1_post_g, l), l)

        pw = pool_w[l].astype(BF16)
        ps = row(pool_scale, l)
        q, kb, vb, ob, kt_stack, vt_stack, tail, qs, ks, vs, us = _inproj(
            xp, xs, row(mix_pre_g, l), w_in_b, pw, ps, stacks, l, batch, seq_len)
        stacks = (kt_stack, vt_stack)
        oa = _prompt_attn(q, kb, vb, sb_bias[l], batch, seq_len)

        bias_rows = jnp.broadcast_to(jnp.repeat(sb_bias[l], dec_seq)[:, None], (N_HEADS * dec_seq, KB))
        dec = _decode(page_table, per_seq(qs), per_seq(ks), per_seq(vs), per_seq(us), st_pad[l],
                      bias_rows, pw.astype(F32), ps, ckt, cvt, l)

        xp, xs = _outproj(xp, xs, oa, ob, dec.reshape(ts, D_MODEL), w_out_b, row(mix_post_g, l), l)

        xp, xs = _ffn(xp, xs, row(ffn2_pre_g, l), *ffn2_w, row(ffn2_post_g, l), l)

        pool_p.append(tail[:, HALO - POOL_BUF:])
        k_s.append(heads(ks))
        v_s.append(heads(vs))
        pool_s.append(jnp.concatenate([state_pool[l], per_seq(us)], axis=1)[:, dec_seq:])

    y_prompt = xp.reshape(batch, seq_len, D_MODEL)
    y_sample = xs.reshape(dec_batch, dec_seq, D_MODEL)
    seq_major = lambda a: a.reshape(depth, batch, N_HEADS, HEAD_DIM, seq_len).transpose(0, 1, 4, 2, 3)
    return (y_prompt, y_sample, seq_major(stacks[0]), seq_major(stacks[1]), jnp.stack(pool_p),
            jnp.stack(k_s), jnp.stack(v_s), jnp.stack(pool_s))
```
